```python
import jax, jax.numpy as jnp
from jax import lax
import numpy as np

D_MODEL = 1024
BATCH = 8
SEQ = 4096
DEPTH = 1
DEC_BATCH = 128
DEC_SEQ = 8
PAST_LEN = 8192
PAGE_SIZE = 128

N_HEADS = 8
HEAD_DIM = 64
N_KV_HEADS = 2
ATTN_WIDTH = N_HEADS * HEAD_DIM
KV_WIDTH = N_KV_HEADS * HEAD_DIM
IDX_HEADS = 8
IDX_DIM = 64
IDX_ROPE_DIM = 32
TOPK_MAX = 256
CONV_CH = D_MODEL - ATTN_WIDTH
CONV_W = 31
D_FF = 2816
FFN_CONV_W = 3
PLE_DIM = 256
ROPE_THETA = 10000.0
EPS = 1e-6
Q_BLOCK = 128
NEG = -1e30

IN_Q = ATTN_WIDTH
IN_K = IN_Q + KV_WIDTH
IN_V = IN_K + KV_WIDTH
IN_QI = IN_V + IDX_HEADS * IDX_DIM
IN_KI = IN_QI + IDX_DIM
IN_WI = IN_KI + IDX_HEADS
D_IN = IN_WI + 2 * CONV_CH

kernel_name = 'hymba_dsa_conformer_convffn_step'


def rmsnorm(x, g):
    xf = x.astype(jnp.float32)
    y = xf * lax.rsqrt(jnp.mean(xf * xf, axis=-1, keepdims=True) + EPS)
    return (y * g.astype(jnp.float32)).astype(x.dtype)


def layernorm(x, g, b):
    xf = x.astype(jnp.float32)
    mu = jnp.mean(xf, axis=-1, keepdims=True)
    xc = xf - mu
    var = jnp.mean(xc * xc, axis=-1, keepdims=True)
    y = xc * lax.rsqrt(var + EPS) * g.astype(jnp.float32) + b.astype(jnp.float32)
    return y.astype(x.dtype)


def rope(x, pos):
    half = x.shape[-1] // 2
    inv = jnp.power(jnp.float32(ROPE_THETA), -jnp.arange(half, dtype=jnp.float32) / half)
    ang = pos.astype(jnp.float32)[:, None] * inv[None, :]
    cos = jnp.cos(ang)[:, None, :]
    sin = jnp.sin(ang)[:, None, :]
    xf = x.astype(jnp.float32)
    x1, x2 = xf[..., :half], xf[..., half:]
    return jnp.concatenate([x1 * cos - x2 * sin, x2 * cos + x1 * sin], axis=-1).astype(x.dtype)


def partial_rope(x, pos):
    return jnp.concatenate([rope(x[..., :IDX_ROPE_DIM], pos), x[..., IDX_ROPE_DIM:]], axis=-1)


def causal_dwconv(x, prev, w, b):
    xp = jnp.concatenate([prev.astype(x.dtype), x], axis=1)
    y = lax.conv_general_dilated(xp, w[:, None, :].astype(x.dtype), window_strides=(1,), padding='VALID',
                                 dimension_numbers=('NWC', 'WIO', 'NWC'), feature_group_count=x.shape[-1])
    return y + b, xp[:, xp.shape[1] - (w.shape[0] - 1):]


def project(hn, pos, w_in, g_ik, b_ik):
    B, T, _ = hn.shape
    z = hn @ w_in
    q, k, v, qi, ki, wi, cu = jnp.split(z, (IN_Q, IN_K, IN_V, IN_QI, IN_KI, IN_WI), axis=-1)
    q = rope(q.reshape(B, T, N_HEADS, HEAD_DIM), pos)
    k = rope(k.reshape(B, T, N_KV_HEADS, HEAD_DIM), pos)
    v = v.reshape(B, T, N_KV_HEADS, HEAD_DIM)
    qi = partial_rope(qi.reshape(B, T, IDX_HEADS, IDX_DIM), pos)
    ki = partial_rope(layernorm(ki, g_ik, b_ik)[:, :, None, :], pos)[:, :, 0]
    wi = wi * (IDX_HEADS ** -0.5 * IDX_DIM ** -0.5)
    glu = cu[..., :CONV_CH] * jax.nn.sigmoid(cu[..., CONV_CH:])
    return q, k, v, qi, ki, wi, glu


def dsa_attend(q, qi, wi, q_pos, k, v, ki, k_pos, topk):
    f32 = jnp.float32
    B, T = q.shape[:2]
    logits = jnp.einsum('bthd,bsd->bths', qi.astype(f32), ki.astype(f32))
    score = jnp.einsum('bths,bth->bts', jax.nn.relu(logits), wi.astype(f32))
    causal = k_pos[None, :] <= q_pos[:, None]
    score = jnp.where(causal[None], score, NEG)
    _, sel = lax.top_k(score, topk)
    sel_valid = k_pos[sel] <= q_pos[None, :, None]
    gather = jax.vmap(lambda a, i: a[i])
    ks = gather(k, sel).astype(f32)
    vs = gather(v, sel).astype(f32)
    qg = q.reshape(B, T, N_KV_HEADS, N_HEADS // N_KV_HEADS, HEAD_DIM).astype(f32)
    s = jnp.einsum('btkgd,btjkd->btkgj', qg, ks) * (HEAD_DIM ** -0.5)
    s = jnp.where(sel_valid[:, :, None, None, :], s, NEG)
    pr = jax.nn.softmax(s, axis=-1)
    o = jnp.einsum('btkgj,btjkd->btkgd', pr, vs)
    return o.reshape(B, T, ATTN_WIDTH).astype(q.dtype)


def prompt_attention(q, qi, wi, k, v, ki, topk):
    B, S = q.shape[:2]
    nb = S // Q_BLOCK
    pos = jnp.arange(S, dtype=jnp.int32)

    def blk(a):
        return jnp.moveaxis(a.reshape((B, nb, Q_BLOCK) + a.shape[2:]), 1, 0)

    def body(args):
        qb, qib, wib, pb = args
        return dsa_attend(qb, qib, wib, pb, k, v, ki, pos, topk)

    out = lax.map(body, (blk(q), blk(qi), blk(wi), pos.reshape(nb, Q_BLOCK)))
    return jnp.moveaxis(out, 0, 1).reshape(B, S, ATTN_WIDTH)


def decoder_layer(h, p, pos, attn_fn, conv_prev, ffn_prev,
                  g_mix, w_in, g_ik, b_ik, w_dw, b_dw, g_cln, b_cln, w_out,
                  g_ffn, w_up, w_fc, b_fc, w_down, g_ple, w_ple_gate, w_ple):
    hn = rmsnorm(h, g_mix)
    q, k, v, qi, ki, wi, glu = project(hn, pos, w_in, g_ik, b_ik)
    a = attn_fn(q, qi, wi, k, v, ki)
    c, conv_state = causal_dwconv(glu, conv_prev, w_dw, b_dw)
    c = jax.nn.silu(layernorm(c, g_cln, b_cln))
    h = h + jnp.concatenate([a, c], axis=-1) @ w_out
    hn = rmsnorm(h, g_ffn)
    u, ffn_state = causal_dwconv(hn @ w_up, ffn_prev, w_fc, b_fc)
    h = h + (jax.nn.silu(u[..., :D_FF]) * u[..., D_FF:]) @ w_down
    gate = jax.nn.sigmoid(rmsnorm(h, g_ple) @ w_ple_gate)
    h = h + (p @ w_ple) * gate
    return h, k, v, ki, conv_state, ffn_state


def setup_inputs(seed: int = 0) -> dict:
    key = jax.random.key(seed)
    ks = jax.random.split(key, 32)
    n_pages = PAST_LEN // PAGE_SIZE
    n_used = DEC_BATCH * n_pages
    n_pool = (n_used * 5) // 4
    f32 = jnp.float32
    nrm = lambda k, shape, s: jax.random.normal(k, shape, f32) * s
    gain = lambda k, shape: 1.0 + 0.02 * jax.random.normal(k, shape, f32)
    page_table = jax.random.permutation(ks[0], n_pool)[:n_used].reshape(DEC_BATCH, n_pages).astype(jnp.int32)
    return {
        'x_prompt': nrm(ks[1], (BATCH, SEQ, D_MODEL), 1.0),
        'x_sample': nrm(ks[2], (DEC_BATCH, DEC_SEQ, D_MODEL), 1.0),
        'p_prompt': nrm(ks[3], (DEPTH, BATCH, SEQ, PLE_DIM), 1.0),
        'p_sample': nrm(ks[4], (DEPTH, DEC_BATCH, DEC_SEQ, PLE_DIM), 1.0),
        'cache_k': nrm(ks[5], (DEPTH, n_pool, PAGE_SIZE, N_KV_HEADS, HEAD_DIM), 1.0),
        'cache_v': nrm(ks[6], (DEPTH, n_pool, PAGE_SIZE, N_KV_HEADS, HEAD_DIM), 1.0),
        'cache_idx_k': nrm(ks[7], (DEPTH, n_pool, PAGE_SIZE, IDX_DIM), 1.0),
        'state_conv': nrm(ks[8], (DEPTH, DEC_BATCH, CONV_W - 1, CONV_CH), 0.5),
        'state_ffn_conv': nrm(ks[9], (DEPTH, DEC_BATCH, FFN_CONV_W - 1, 2 * D_FF), 1.0),
        'page_table': page_table,
        'g_mix': gain(ks[10], (DEPTH, D_MODEL)),
        'w_in': nrm(ks[11], (DEPTH, D_MODEL, D_IN), D_MODEL ** -0.5),
        'g_idx_k': gain(ks[12], (DEPTH, IDX_DIM)),
        'b_idx_k': nrm(ks[13], (DEPTH, IDX_DIM), 0.02),
        'w_dw': nrm(ks[14], (DEPTH, CONV_W, CONV_CH), CONV_W ** -0.5),
        'b_dw': nrm(ks[15], (DEPTH, CONV_CH), 0.02),
        'g_conv_ln': gain(ks[16], (DEPTH, CONV_CH)),
        'b_conv_ln': nrm(ks[17], (DEPTH, CONV_CH), 0.02),
        'w_out': nrm(ks[18], (DEPTH, D_MODEL, D_MODEL), D_MODEL ** -0.5),
        'g_ffn': gain(ks[19], (DEPTH, D_MODEL)),
        'w_up': nrm(ks[20], (DEPTH, D_MODEL, 2 * D_FF), D_MODEL ** -0.5),
        'w_ffn_conv': nrm(ks[21], (DEPTH, FFN_CONV_W, 2 * D_FF), FFN_CONV_W ** -0.5),
        'b_ffn_conv': nrm(ks[22], (DEPTH, 2 * D_FF), 0.02),
        'w_down': nrm(ks[23], (DEPTH, D_FF, D_MODEL), D_FF ** -0.5),
        'g_ple': gain(ks[24], (DEPTH, D_MODEL)),
        'w_ple_gate': nrm(ks[25], (DEPTH, D_MODEL, D_MODEL), D_MODEL ** -0.5),
        'w_ple': nrm(ks[26], (DEPTH, PLE_DIM, D_MODEL), PLE_DIM ** -0.5),
        'g_final': gain(ks[27], (D_MODEL,)),
    }


def reference(x_prompt, x_sample, p_prompt, p_sample, cache_k, cache_v, cache_idx_k, state_conv, state_ffn_conv,
              page_table, g_mix, w_in, g_idx_k, b_idx_k, w_dw, b_dw, g_conv_ln, b_conv_ln, w_out, g_ffn, w_up,
              w_ffn_conv, b_ffn_conv, w_down, g_ple, w_ple_gate, w_ple, g_final):
    B, S = x_prompt.shape[:2]
    Bd, Td = x_sample.shape[:2]
    past_len = page_table.shape[1] * cache_k.shape[2]
    topk_prompt = min(TOPK_MAX, S // 4)
    topk_sample = min(TOPK_MAX, (past_len + Td) // 4)
    pos_prompt = jnp.arange(S, dtype=jnp.int32)
    pos_sample = past_len + jnp.arange(Td, dtype=jnp.int32)
    key_pos_sample = jnp.arange(past_len + Td, dtype=jnp.int32)

    hp, hs = x_prompt, x_sample
    kp_l, vp_l, ip_l, cp_l, fp_l = [], [], [], [], []
    ks_l, vs_l, is_l, cs_l, fs_l = [], [], [], [], []
    for i in range(DEPTH):
        prm = (g_mix[i], w_in[i], g_idx_k[i], b_idx_k[i], w_dw[i], b_dw[i], g_conv_ln[i], b_conv_ln[i], w_out[i],
               g_ffn[i], w_up[i], w_ffn_conv[i], b_ffn_conv[i], w_down[i], g_ple[i], w_ple_gate[i], w_ple[i])

        def prompt_attn(q, qi, wi, k, v, ki):
            return prompt_attention(q, qi, wi, k, v, ki, topk_prompt)

        def sample_attn(q, qi, wi, k, v, ki, i=i):
            k_past = cache_k[i][page_table].reshape(Bd, past_len, N_KV_HEADS, HEAD_DIM)
            v_past = cache_v[i][page_table].reshape(Bd, past_len, N_KV_HEADS, HEAD_DIM)
            i_past = cache_idx_k[i][page_table].reshape(Bd, past_len, IDX_DIM)
            kk = jnp.concatenate([k_past.astype(k.dtype), k], axis=1)
            vv = jnp.concatenate([v_past.astype(v.dtype), v], axis=1)
            kki = jnp.concatenate([i_past.astype(ki.dtype), ki], axis=1)
            return dsa_attend(q, qi, wi, pos_sample, kk, vv, kki, key_pos_sample, topk_sample)

        conv0 = jnp.zeros((B, CONV_W - 1, CONV_CH), x_prompt.dtype)
        ffn0 = jnp.zeros((B, FFN_CONV_W - 1, 2 * D_FF), x_prompt.dtype)
        hp, kp, vp, ip, cp, fp = decoder_layer(hp, p_prompt[i], pos_prompt, prompt_attn, conv0, ffn0, *prm)
        hs, kn, vn, inn, cn, fn = decoder_layer(hs, p_sample[i], pos_sample, sample_attn,
                                                state_conv[i], state_ffn_conv[i], *prm)
        kp_l.append(kp); vp_l.append(vp); ip_l.append(ip); cp_l.append(cp); fp_l.append(fp)
        ks_l.append(kn); vs_l.append(vn); is_l.append(inn); cs_l.append(cn); fs_l.append(fn)

    y_prompt = rmsnorm(hp, g_final)
    y_sample = rmsnorm(hs, g_final)
    return (y_prompt, y_sample,
            jnp.stack(kp_l), jnp.stack(vp_l), jnp.stack(ip_l), jnp.stack(cp_l), jnp.stack(fp_l),
            jnp.stack(ks_l), jnp.stack(vs_l), jnp.stack(is_l), jnp.stack(cs_l), jnp.stack(fs_l))
```

```python
import functools

import jax
import jax.numpy as jnp
from jax import lax
from jax.experimental import pallas as pl
from jax.experimental.pallas import tpu as pltpu

N_HEADS = 8
HEAD_DIM = 64
N_KV_HEADS = 2
GROUP = N_HEADS // N_KV_HEADS
ATTN_WIDTH = N_HEADS * HEAD_DIM
KV_WIDTH = N_KV_HEADS * HEAD_DIM
IDX_HEADS = 8
IDX_DIM = 64
IDX_ROPE_DIM = 32
TOPK_MAX = 256
CONV_W = 31
FFN_CONV_W = 3
ROPE_THETA = 10000.0
EPS = 1e-6
NEG = -1e30

LANES = 128
SUBLANES = 8
INT_MIN = -2 ** 31
VMEM_LIMIT = 56 * 1024 * 1024

F32 = jnp.float32
BF16 = jnp.bfloat16
I32 = jnp.int32

_NT = (((1,), (1,)), ((), ()))


def _cparams(sem):
    return pltpu.CompilerParams(dimension_semantics=sem, vmem_limit_bytes=VMEM_LIMIT)


def _sortable(x):
    b = pltpu.bitcast(x, I32)
    return b ^ ((b >> 31) & jnp.int32(0x7FFFFFFF))


def _rmsnorm(x, g):
    return x * lax.rsqrt(jnp.mean(x * x, axis=-1, keepdims=True) + EPS) * g


C_Q = 0
C_K = C_Q + ATTN_WIDTH
C_V = C_K + KV_WIDTH
C_QI = C_V + KV_WIDTH
C_KI = C_QI + IDX_HEADS * IDX_DIM
C_A = C_KI + LANES
C_G = C_A + 512
C_END = C_G + 512
CONV_CH = 512


def _proj_kernel(x_ref, g_ref, w_ref, gik_ref, bik_ref, cq_ref, sq_ref, ci_ref, si_ref,
                 q_ref, k_ref, v_ref, kb_ref, vb_ref, qi_ref, ki_ref, kib_ref, wi_ref, wit_ref, glu_ref,
                 *, wi_scale):
    tm = x_ref.shape[0]
    hn = _rmsnorm(x_ref[...], g_ref[...]).astype(BF16)
    z = jnp.dot(hn, w_ref[...], preferred_element_type=F32)

    lane = lax.broadcasted_iota(I32, (tm, LANES), 1)
    in_head = lane % HEAD_DIM
    cq, sq, ci, si = cq_ref[...], sq_ref[...], ci_ref[...], si_ref[...]

    def rope_full(xg):
        sw = jnp.where(in_head < HEAD_DIM // 2, pltpu.roll(xg, LANES - HEAD_DIM // 2, 1),
                       pltpu.roll(xg, HEAD_DIM // 2, 1))
        return xg * cq + sw * sq

    def rope_part(xg):
        sw = jnp.where(in_head < IDX_ROPE_DIM // 2, pltpu.roll(xg, LANES - IDX_ROPE_DIM // 2, 1),
                       pltpu.roll(xg, IDX_ROPE_DIM // 2, 1))
        return xg * ci + sw * si

    for g in range(ATTN_WIDTH // LANES):
        qg = rope_full(z[:, C_Q + g * LANES:C_Q + (g + 1) * LANES])
        q_ref[:, g * LANES:(g + 1) * LANES] = (qg * (HEAD_DIM ** -0.5)).astype(BF16)
        qig = rope_part(z[:, C_QI + g * LANES:C_QI + (g + 1) * LANES])
        qi_ref[:, g * LANES:(g + 1) * LANES] = qig.astype(BF16)

    kr = rope_full(z[:, C_K:C_K + LANES])
    k_ref[...] = kr
    vr = z[:, C_V:C_V + LANES]
    v_ref[...] = vr
    for g in range(N_KV_HEADS):
        kb_ref[g] = kr[:, g * HEAD_DIM:(g + 1) * HEAD_DIM].astype(BF16)
        vb_ref[g] = vr[:, g * HEAD_DIM:(g + 1) * HEAD_DIM].astype(BF16)

    zg = z[:, C_KI:C_KI + LANES]
    is_ki = lane < IDX_DIM
    mu = jnp.sum(jnp.where(is_ki, zg, 0.0), axis=-1, keepdims=True) / IDX_DIM
    xc = jnp.where(is_ki, zg - mu, 0.0)
    var = jnp.sum(xc * xc, axis=-1, keepdims=True) / IDX_DIM
    kin = rope_part(xc * lax.rsqrt(var + EPS) * gik_ref[...] + bik_ref[...])
    ki_ref[...] = kin[:, :IDX_DIM]
    kib_ref[...] = kin[:, :IDX_DIM].astype(BF16)

    wig = zg * wi_scale
    wi_ref[...] = wig[:, IDX_DIM:IDX_DIM + IDX_HEADS]
    wit_ref[...] = wig.T[IDX_DIM:IDX_DIM + IDX_HEADS, :]

    glu_ref[...] = z[:, C_A:C_A + CONV_CH] * jax.nn.sigmoid(z[:, C_G:C_G + CONV_CH])


def _rope_tables(pos):
    def cs(half):
        inv = jnp.power(jnp.float32(ROPE_THETA), -jnp.arange(half, dtype=F32) / half)
        ang = pos.astype(F32)[:, None] * inv[None, :]
        return jnp.cos(ang), jnp.sin(ang)

    c, s = cs(HEAD_DIM // 2)
    cq = jnp.concatenate([c, c, c, c], axis=-1)
    sq = jnp.concatenate([-s, s, -s, s], axis=-1)
    c2, s2 = cs(IDX_ROPE_DIM // 2)
    one = jnp.ones((pos.shape[0], IDX_DIM - IDX_ROPE_DIM), F32)
    ci = jnp.concatenate([c2, c2, one, c2, c2, one], axis=-1)
    si = jnp.concatenate([-s2, s2, 0 * one, -s2, s2, 0 * one], axis=-1)
    return cq, sq, ci, si


def _project(x, tables, tab_blocks, g_mix, w_comb, gik, bik, tm):
    n, d = x.shape
    nb = n // tm
    row = lambda i: (i, 0)
    const = lambda i: (0, 0)
    tab = lambda i: (i % tab_blocks, 0)
    wi_scale = IDX_HEADS ** -0.5 * IDX_DIM ** -0.5
    out_shape = (
        jax.ShapeDtypeStruct((n, ATTN_WIDTH), BF16),
        jax.ShapeDtypeStruct((n, KV_WIDTH), F32),
        jax.ShapeDtypeStruct((n, KV_WIDTH), F32),
        jax.ShapeDtypeStruct((N_KV_HEADS, n, HEAD_DIM), BF16),
        jax.ShapeDtypeStruct((N_KV_HEADS, n, HEAD_DIM), BF16),
        jax.ShapeDtypeStruct((n, IDX_HEADS * IDX_DIM), BF16),
        jax.ShapeDtypeStruct((n, IDX_DIM), F32),
        jax.ShapeDtypeStruct((n, IDX_DIM), BF16),
        jax.ShapeDtypeStruct((n, IDX_HEADS), F32),
        jax.ShapeDtypeStruct((IDX_HEADS, n), F32),
        jax.ShapeDtypeStruct((n, CONV_CH), F32),
    )
    out_specs = (
        pl.BlockSpec((tm, ATTN_WIDTH), row),
        pl.BlockSpec((tm, KV_WIDTH), row),
        pl.BlockSpec((tm, KV_WIDTH), row),
        pl.BlockSpec((N_KV_HEADS, tm, HEAD_DIM), lambda i: (0, i, 0)),
        pl.BlockSpec((N_KV_HEADS, tm, HEAD_DIM), lambda i: (0, i, 0)),
        pl.BlockSpec((tm, IDX_HEADS * IDX_DIM), row),
        pl.BlockSpec((tm, IDX_DIM), row),
        pl.BlockSpec((tm, IDX_DIM), row),
        pl.BlockSpec((tm, IDX_HEADS), row),
        pl.BlockSpec((IDX_HEADS, tm), lambda i: (0, i)),
        pl.BlockSpec((tm, CONV_CH), row),
    )
    in_specs = [
        pl.BlockSpec((tm, d), row),
        pl.BlockSpec((1, d), const),
        pl.BlockSpec((d, C_END), const),
        pl.BlockSpec((1, LANES), const),
        pl.BlockSpec((1, LANES), const),
    ] + [pl.BlockSpec((tm, LANES), tab)] * 4
    return pl.pallas_call(
        functools.partial(_proj_kernel, wi_scale=wi_scale),
        grid=(nb,), in_specs=in_specs, out_specs=out_specs, out_shape=out_shape,
        compiler_params=_cparams(("parallel",)), name="in_proj",
    )(x, g_mix, w_comb, gik, bik, *tables)


def _count_ge(key_ref, nkt, cand):
    def body(kt, acc):
        m = (key_ref[kt] >= cand).astype(I32)
        return acc + jnp.sum(m.reshape(LANES // SUBLANES, SUBLANES, LANES), axis=0)
    acc = lax.fori_loop(0, nkt, body, jnp.zeros((SUBLANES, LANES), I32))
    return jnp.sum(acc, axis=0, keepdims=True)


def _prompt_attn_kernel(qi_ref, wit_ref, q_ref, kib_ref, kb_ref, vb_ref, a_ref,
                        key_ref, bias_ref, s_ref, m_ref, l_ref, *, topk):
    i = pl.program_id(1)
    nkt = i + 1
    tq = LANES
    r_io = lax.broadcasted_iota(I32, (tq, tq), 0)
    c_io = lax.broadcasted_iota(I32, (tq, tq), 1)

    qi = qi_ref[...]
    qi_pairs = [jnp.concatenate([qi[:, (2 * hp) * IDX_DIM:(2 * hp + 1) * IDX_DIM],
                                 qi[:, (2 * hp + 1) * IDX_DIM:(2 * hp + 2) * IDX_DIM]], axis=0)
                for hp in range(IDX_HEADS // 2)]
    w_rows = [wit_ref[h:h + 1, :] for h in range(IDX_HEADS)]

    def score_body(kt, carry):
        kit = kib_ref[pl.ds(pl.multiple_of(kt * tq, tq), tq), :]
        sc = jnp.zeros((tq, tq), F32)
        for hp in range(IDX_HEADS // 2):
            lg = lax.dot_general(kit, qi_pairs[hp], _NT, preferred_element_type=F32)
            sc = sc + jnp.maximum(lg[:, :tq], 0.0) * w_rows[2 * hp]
            sc = sc + jnp.maximum(lg[:, tq:], 0.0) * w_rows[2 * hp + 1]
        causal = (kt < i) | (r_io <= c_io)
        key_ref[kt] = jnp.where(causal, _sortable(sc), INT_MIN)
        return carry
    lax.fori_loop(0, nkt, score_body, 0)

    t0 = jnp.where(_count_ge(key_ref, nkt, jnp.zeros((1, tq), I32)) >= topk, 0, INT_MIN).astype(I32)

    def bit_body(p, t):
        cand = t | jnp.left_shift(jnp.int32(1), 30 - p)
        return jnp.where(_count_ge(key_ref, nkt, cand) >= topk, cand, t)
    thr = lax.fori_loop(0, 31, bit_body, t0)

    def tie_counts(kt, c):
        k = key_ref[kt]
        gt = jnp.sum((k > thr).astype(I32).reshape(LANES // SUBLANES, SUBLANES, LANES), axis=0)
        eq = jnp.sum((k == thr).astype(I32).reshape(LANES // SUBLANES, SUBLANES, LANES), axis=0)
        return c[0] + gt, c[1] + eq
    z8 = jnp.zeros((SUBLANES, LANES), I32)
    n_gt, n_eq = lax.fori_loop(0, nkt, tie_counts, (z8, z8))
    n_gt = jnp.sum(n_gt, axis=0, keepdims=True)
    n_eq = jnp.sum(n_eq, axis=0, keepdims=True)
    need = topk - n_gt

    def eq_below(cand):
        def body(kt, acc):
            m = ((key_ref[kt] == thr) & (r_io + kt * tq < cand)).astype(I32)
            return acc + jnp.sum(m.reshape(LANES // SUBLANES, SUBLANES, LANES), axis=0)
        return jnp.sum(lax.fori_loop(0, nkt, body, z8), axis=0, keepdims=True)

    n_idx_bits = max(1, (key_ref.shape[0] * tq).bit_length())

    def tie_search(_):
        def body(p, m):
            cand = m | jnp.left_shift(jnp.int32(1), n_idx_bits - 1 - p)
            return jnp.where(eq_below(cand) < need, cand, m)
        return lax.fori_loop(0, n_idx_bits, body, jnp.zeros((1, tq), I32))

    excess = jnp.max(jnp.where(need > 0, n_eq - need, 0)) > 0
    idx_max = lax.cond(excess, tie_search,
                       lambda _: jnp.full((1, tq), key_ref.shape[0] * tq, I32), 0)

    def bias_body(kt, carry):
        k = key_ref[kt]
        sel = (k > thr) | ((k == thr) & (r_io + kt * tq <= idx_max))
        sel = sel & ((kt < i) | (r_io <= c_io))
        bias_ref[kt] = jnp.where(sel, 0.0, NEG).astype(F32).T
        return carry
    lax.fori_loop(0, nkt, bias_body, 0)

    q = q_ref[...]
    for g in range(N_KV_HEADS):
        q4 = jnp.concatenate([q[:, (GROUP * g + hh) * HEAD_DIM:(GROUP * g + hh + 1) * HEAD_DIM]
                              for hh in range(GROUP)], axis=0)
        m_ref[...] = jnp.full(m_ref.shape, NEG, F32)

        def qk_body(kt, carry):
            kt_tile = kb_ref[g, pl.ds(pl.multiple_of(kt * tq, tq), tq), :]
            s = lax.dot_general(q4, kt_tile, _NT, preferred_element_type=F32)
            s = (s.reshape(GROUP, tq, tq) + bias_ref[kt][None]).reshape(GROUP * tq, tq)
            s_ref[kt] = s
            m_ref[...] = jnp.maximum(m_ref[...], s)
            return carry
        lax.fori_loop(0, nkt, qk_body, 0)
        m = jnp.max(m_ref[...], axis=-1, keepdims=True)
        l_ref[...] = jnp.zeros(l_ref.shape, F32)

        def pv_body(kt, acc):
            p = jnp.exp(s_ref[kt] - m)
            l_ref[...] += p
            vt_tile = vb_ref[g, pl.ds(pl.multiple_of(kt * tq, tq), tq), :]
            return acc + jnp.dot(p.astype(BF16), vt_tile, preferred_element_type=F32)
        acc = lax.fori_loop(0, nkt, pv_body, jnp.zeros((GROUP * tq, HEAD_DIM), F32))
        o = acc / jnp.sum(l_ref[...], axis=-1, keepdims=True)
        for hh in range(GROUP):
            h = GROUP * g + hh
            a_ref[:, h * HEAD_DIM:(h + 1) * HEAD_DIM] = o[hh * tq:(hh + 1) * tq, :].astype(BF16)


def _prompt_attention(q, qi, wit, kib, kb, vb, batch, seq, topk):
    tq = LANES
    nb = seq // tq
    qrow = lambda b, i: (b * nb + i, 0)
    return pl.pallas_call(
        functools.partial(_prompt_attn_kernel, topk=topk),
        grid=(batch, nb),
        in_specs=[
            pl.BlockSpec((tq, IDX_HEADS * IDX_DIM), qrow),
            pl.BlockSpec((IDX_HEADS, tq), lambda b, i: (0, b * nb + i)),
            pl.BlockSpec((tq, ATTN_WIDTH), qrow),
            pl.BlockSpec((seq, IDX_DIM), lambda b, i: (b, 0)),
            pl.BlockSpec((N_KV_HEADS, seq, HEAD_DIM), lambda b, i: (0, b, 0)),
            pl.BlockSpec((N_KV_HEADS, seq, HEAD_DIM), lambda b, i: (0, b, 0)),
        ],
        out_specs=pl.BlockSpec((tq, ATTN_WIDTH), qrow),
        out_shape=jax.ShapeDtypeStruct((batch * seq, ATTN_WIDTH), BF16),
        scratch_shapes=[
            pltpu.VMEM((nb, tq, tq), I32),
            pltpu.VMEM((nb, tq, tq), F32),
            pltpu.VMEM((nb, GROUP * tq, tq), F32),
            pltpu.VMEM((GROUP * tq, tq), F32),
            pltpu.VMEM((GROUP * tq, tq), F32),
        ],
        compiler_params=_cparams(("parallel", "arbitrary")), name="prompt_dsa",
    )(qi, wit, q, kib, kb, vb)


def _conv_out_kernel(glu_ref, halo_ref, a_ref, x_ref, wdw_ref, bdw_ref, gln_ref, bln_ref, wa_ref, wc_ref,
                     h_ref, xp_ref, c_ref, *, shift, zero_first, rc):
    tm = glu_ref.shape[0]
    halo = halo_ref.shape[0]
    hv = halo_ref[...]
    if zero_first:
        hv = jnp.where(pl.program_id(1) == 0, 0.0, hv)
    xp_ref[0:halo, :] = hv
    xp_ref[halo:halo + tm, :] = glu_ref[...]

    for lg in range(CONV_CH // LANES):
        ls = slice(lg * LANES, (lg + 1) * LANES)
        for r0 in range(0, tm, rc):
            acc = jnp.zeros((rc, LANES), F32)
            for j in range(CONV_W):
                off = halo - (CONV_W - 1 - j) * shift
                acc = acc + wdw_ref[j:j + 1, ls] * xp_ref[r0 + off:r0 + off + rc, ls]
            c_ref[r0:r0 + rc, ls] = acc + bdw_ref[:, ls]

    c = c_ref[...]
    mu = jnp.mean(c, axis=-1, keepdims=True)
    xc = c - mu
    var = jnp.mean(xc * xc, axis=-1, keepdims=True)
    y = xc * lax.rsqrt(var + EPS) * gln_ref[...] + bln_ref[...]
    y = jax.nn.silu(y)
    h_ref[...] = (x_ref[...]
                  + jnp.dot(a_ref[...], wa_ref[...], preferred_element_type=F32)
                  + jnp.dot(y.astype(BF16), wc_ref[...], preferred_element_type=F32))


def _conv_out(glu, halo_arr, halo_spec, a, x, w_dw, b_dw, g_ln, b_ln, w_a, w_c, *, groups, nt, tm,
              shift, zero_first):
    d = x.shape[1]
    halo = halo_spec.block_shape[0]
    row = lambda b, i: (b * nt + i, 0)
    const = lambda b, i: (0, 0)
    return pl.pallas_call(
        functools.partial(_conv_out_kernel, shift=shift, zero_first=zero_first, rc=32),
        grid=(groups, nt),
        in_specs=[
            pl.BlockSpec((tm, CONV_CH), row),
            halo_spec,
            pl.BlockSpec((tm, ATTN_WIDTH), row),
            pl.BlockSpec((tm, d), row),
            pl.BlockSpec((CONV_W, CONV_CH), const),
            pl.BlockSpec((1, CONV_CH), const),
            pl.BlockSpec((1, CONV_CH), const),
            pl.BlockSpec((1, CONV_CH), const),
            pl.BlockSpec((ATTN_WIDTH, d), const),
            pl.BlockSpec((CONV_CH, d), const),
        ],
        out_specs=pl.BlockSpec((tm, d), row),
        out_shape=jax.ShapeDtypeStruct(x.shape, F32),
        scratch_shapes=[pltpu.VMEM((halo + tm, CONV_CH), F32), pltpu.VMEM((tm, CONV_CH), F32)],
        compiler_params=_cparams(("parallel", "arbitrary")), name="conv_out_proj",
    )(glu, halo_arr, a, x, w_dw, b_dw, g_ln, b_ln, w_a, w_c)


FFN_CW = 256


def _ffn_kernel(h_ref, p_ref, st_ref, gffn_ref, wug_ref, wuv_ref, wfg_ref, wfv_ref, bfg_ref, bfv_ref,
                wdn_ref, gple_ref, wgate_ref, wple_ref, gfin_ref,
                y_ref, sto_ref, ug_ref, uv_ref, carry_ref, acc_ref, *, shift, halo, final):
    tm = h_ref.shape[0]
    nch = wug_ref.shape[0]
    first = pl.program_id(1) == 0
    h = h_ref[...]
    hn = _rmsnorm(h, gffn_ref[...]).astype(BF16)
    acc_ref[...] = jnp.zeros(acc_ref.shape, F32)

    def chunk(c, carry):
        def conv(u_ref, wu_ref, wf_ref, bf_ref, slot):
            u = jnp.dot(hn, wu_ref[c], preferred_element_type=F32)

            @pl.when(first)
            def _():
                u_ref[0:halo, :] = st_ref[slot]

            @pl.when(jnp.logical_not(first))
            def _():
                u_ref[0:halo, :] = carry_ref[slot]
            u_ref[halo:halo + tm, :] = u
            tail = u[tm - halo:, :]
            carry_ref[slot] = tail
            sto_ref[slot] = tail
            wf = wf_ref[c]
            out = bf_ref[c] + wf[2:3, :] * u
            for k in range(FFN_CONV_W - 1):
                off = halo - (FFN_CONV_W - 1 - k) * shift
                out = out + wf[k:k + 1, :] * u_ref[pl.ds(off, tm), :]
            return out
        cg = conv(ug_ref, wug_ref, wfg_ref, bfg_ref, c)
        cv = conv(uv_ref, wuv_ref, wfv_ref, bfv_ref, nch + c)
        act = (jax.nn.silu(cg) * cv).astype(BF16)
        acc_ref[...] += jnp.dot(act, wdn_ref[c], preferred_element_type=F32)
        return carry
    lax.fori_loop(0, nch, chunk, 0)

    h2 = h + acc_ref[...]
    gate = jax.nn.sigmoid(jnp.dot(_rmsnorm(h2, gple_ref[...]).astype(BF16), wgate_ref[...],
                                  preferred_element_type=F32))
    ple = jnp.dot(p_ref[...].astype(BF16), wple_ref[...], preferred_element_type=F32)
    h3 = h2 + ple * gate
    y_ref[...] = _rmsnorm(h3, gfin_ref[...]) if final else h3


def _ffn(h, p, st, wts, *, groups, nt, tm, shift, halo, final):
    (g_ffn, wug, wuv, wfg, wfv, bfg, bfv, wdn, g_ple, w_gate, w_ple, g_fin) = wts
    d = h.shape[1]
    nch, _, cw = wug.shape
    row = lambda b, i: (b * nt + i, 0)
    c2 = lambda b, i: (0, 0)
    c3 = lambda b, i: (0, 0, 0)
    stspec = pl.BlockSpec((None, 2 * nch, halo, cw), lambda b, i: (b, 0, 0, 0))
    once = dict(pipeline_mode=pl.Buffered(1))
    return pl.pallas_call(
        functools.partial(_ffn_kernel, shift=shift, halo=halo, final=final),
        grid=(groups, nt),
        in_specs=[
            pl.BlockSpec((tm, d), row),
            pl.BlockSpec((tm, p.shape[1]), row),
            stspec,
            pl.BlockSpec((1, d), c2),
            pl.BlockSpec(wug.shape, c3, **once),
            pl.BlockSpec(wuv.shape, c3, **once),
            pl.BlockSpec(wfg.shape, c3),
            pl.BlockSpec(wfv.shape, c3),
            pl.BlockSpec(bfg.shape, c3),
            pl.BlockSpec(bfv.shape, c3),
            pl.BlockSpec(wdn.shape, c3, **once),
            pl.BlockSpec((1, d), c2),
            pl.BlockSpec(w_gate.shape, c2, **once),
            pl.BlockSpec(w_ple.shape, c2, **once),
            pl.BlockSpec((1, d), c2),
        ],
        out_specs=(pl.BlockSpec((tm, d), row), stspec),
        out_shape=(jax.ShapeDtypeStruct(h.shape, F32),
                   jax.ShapeDtypeStruct((groups, 2 * nch, halo, cw), F32)),
        scratch_shapes=[
            pltpu.VMEM((halo + tm, cw), F32),
            pltpu.VMEM((halo + tm, cw), F32),
            pltpu.VMEM((2 * nch, halo, cw), F32),
            pltpu.VMEM((tm, d), F32),
        ],
        compiler_params=_cparams(("parallel", "arbitrary")), name="conv_ffn_ple",
    )(h, p, st, g_ffn, wug, wuv, wfg, wfv, bfg, bfv, wdn, g_ple, w_gate, w_ple, g_fin)


def _start_pages(pt_ref, b, src_hbm, dst_ref, slot, sem):
    for p in range(dst_ref.shape[1]):
        pltpu.make_async_copy(src_hbm.at[pt_ref[b, p]], dst_ref.at[slot, p], sem.at[slot]).start()


def _wait_pages(src_hbm, dst_ref, slot, sem):
    for p in range(dst_ref.shape[1]):
        pltpu.make_async_copy(src_hbm.at[0], dst_ref.at[slot, p], sem.at[slot]).wait()


def _rows_by_head(x, width):
    return jnp.concatenate([x[:, h * width:(h + 1) * width] for h in range(x.shape[1] // width)], axis=0)


def _sample_score_kernel(pt_ref, qi_ref, wi_ref, kin_ref, cidx_hbm, key_ref, ibuf, sem, *, chunk_pages):
    b = pl.program_id(0)
    nb = pl.num_programs(0)
    slot = b % 2
    npages, page = ibuf.shape[1], ibuf.shape[2]
    td = qi_ref.shape[0]

    @pl.when(b == 0)
    def _():
        _start_pages(pt_ref, b, cidx_hbm, ibuf, 0, sem)

    @pl.when(b + 1 < nb)
    def _():
        _start_pages(pt_ref, b + 1, cidx_hbm, ibuf, 1 - slot, sem)

    qi = _rows_by_head(qi_ref[...], IDX_DIM)
    wcol = jnp.concatenate([wi_ref[:, h:h + 1] for h in range(IDX_HEADS)], axis=0)

    def score(keys_bf):
        lg = lax.dot_general(qi, keys_bf, _NT, preferred_element_type=F32)
        r = jnp.maximum(lg, 0.0) * wcol
        return jnp.sum(r.reshape(IDX_HEADS, td, r.shape[1]), axis=0)

    _wait_pages(cidx_hbm, ibuf, slot, sem)
    cw = chunk_pages * page
    for c in range(npages // chunk_pages):
        kc = ibuf[slot, c * chunk_pages:(c + 1) * chunk_pages].reshape(cw, IDX_DIM).astype(BF16)
        key_ref[:, c * cw:(c + 1) * cw] = _sortable(score(kc))

    kn = jnp.concatenate([kin_ref[...], jnp.zeros((LANES - td, IDX_DIM), BF16)], axis=0)
    sc = score(kn)
    t_io = lax.broadcasted_iota(I32, sc.shape, 0)
    j_io = lax.broadcasted_iota(I32, sc.shape, 1)
    key_ref[:, npages * page:] = jnp.where(j_io <= t_io, _sortable(sc), INT_MIN)


def _sample_scores(page_table, qi, wi, kib, cache_idx, td, chunk_pages=4):
    bd, npages = page_table.shape
    page = cache_idx.shape[1]
    nk = npages * page + LANES
    grid_spec = pltpu.PrefetchScalarGridSpec(
        num_scalar_prefetch=1, grid=(bd,),
        in_specs=[
            pl.BlockSpec((td, IDX_HEADS * IDX_DIM), lambda b, pt: (b, 0)),
            pl.BlockSpec((td, IDX_HEADS), lambda b, pt: (b, 0)),
            pl.BlockSpec((td, IDX_DIM), lambda b, pt: (b, 0)),
            pl.BlockSpec(memory_space=pl.ANY),
        ],
        out_specs=pl.BlockSpec((td, nk), lambda b, pt: (b, 0)),
        scratch_shapes=[pltpu.VMEM((2, npages, page, IDX_DIM), F32), pltpu.SemaphoreType.DMA((2,))],
    )
    return pl.pallas_call(
        functools.partial(_sample_score_kernel, chunk_pages=chunk_pages),
        grid_spec=grid_spec, out_shape=jax.ShapeDtypeStruct((bd * td, nk), I32),
        compiler_params=_cparams(("arbitrary",)), name="sample_scores",
    )(page_table, qi, wi, kib, cache_idx)


def _select_kernel(key_ref, bias_ref, *, topk):
    rows, nk = key_ref.shape
    ntile = nk // LANES

    def count(pred):
        acc = jnp.zeros((rows, LANES), I32)
        for c in range(ntile):
            acc = acc + pred(key_ref[:, c * LANES:(c + 1) * LANES], c).astype(I32)
        return jnp.sum(acc, axis=-1, keepdims=True)

    t0 = jnp.where(count(lambda k, c: k >= 0) >= topk, 0, INT_MIN).astype(I32)

    def bit_body(p, t):
        cand = t | jnp.left_shift(jnp.int32(1), 30 - p)
        return jnp.where(count(lambda k, c: k >= cand) >= topk, cand, t)
    thr = lax.fori_loop(0, 31, bit_body, t0)

    n_gt = count(lambda k, c: k > thr)
    n_eq = count(lambda k, c: k == thr)
    need = topk - n_gt
    l_io = lax.broadcasted_iota(I32, (rows, LANES), 1)
    n_idx_bits = max(1, nk.bit_length())

    def tie_search(_):
        def body(p, m):
            cand = m | jnp.left_shift(jnp.int32(1), n_idx_bits - 1 - p)
            below = count(lambda k, c: (k == thr) & (l_io + c * LANES < cand))
            return jnp.where(below < need, cand, m)
        return lax.fori_loop(0, n_idx_bits, body, jnp.zeros((rows, 1), I32))

    excess = jnp.max(jnp.where(need > 0, n_eq - need, 0)) > 0
    idx_max = lax.cond(excess, tie_search, lambda _: jnp.full((rows, 1), nk, I32), 0)

    for c in range(ntile):
        k = key_ref[:, c * LANES:(c + 1) * LANES]
        sel = ((k > thr) | ((k == thr) & (l_io + c * LANES <= idx_max))) & (k != INT_MIN)
        bias_ref[:, c * LANES:(c + 1) * LANES] = jnp.where(sel, 0.0, NEG).astype(F32)


def _select(keys, topk, rows):
    n, nk = keys.shape
    return pl.pallas_call(
        functools.partial(_select_kernel, topk=topk),
        grid=(n // rows,),
        in_specs=[pl.BlockSpec((rows, nk), lambda i: (i, 0))],
        out_specs=pl.BlockSpec((rows, nk), lambda i: (i, 0)),
        out_shape=jax.ShapeDtypeStruct((n, nk), F32),
        compiler_params=_cparams(("parallel",)), name="sample_select",
    )(keys)


def _sample_attn_kernel(pt_ref, q_ref, bias_ref, kn_ref, vn_ref, ck_hbm, cv_hbm, a_ref,
                        kbuf, vbuf, s_ref, ksem, vsem, *, chunk_pages):
    b = pl.program_id(0)
    nb = pl.num_programs(0)
    slot = b % 2
    npages, page = kbuf.shape[1], kbuf.shape[2]
    td = q_ref.shape[0]
    rows = N_HEADS * td
    half = GROUP * td

    @pl.when(b == 0)
    def _():
        _start_pages(pt_ref, b, ck_hbm, kbuf, 0, ksem)
        _start_pages(pt_ref, b, cv_hbm, vbuf, 0, vsem)

    @pl.when(b + 1 < nb)
    def _():
        _start_pages(pt_ref, b + 1, ck_hbm, kbuf, 1 - slot, ksem)
        _start_pages(pt_ref, b + 1, cv_hbm, vbuf, 1 - slot, vsem)

    q = _rows_by_head(q_ref[...], HEAD_DIM)
    cw = chunk_pages * page
    nchunk = npages // chunk_pages

    def logits(kc):
        kc = kc.astype(BF16)
        return jnp.concatenate(
            [lax.dot_general(q[g * half:(g + 1) * half], kc[:, g * HEAD_DIM:(g + 1) * HEAD_DIM], _NT,
                             preferred_element_type=F32) for g in range(N_KV_HEADS)], axis=0)

    def tile_bias(lo, n):
        return jnp.concatenate([bias_ref[:, lo:lo + n]] * N_HEADS, axis=0)

    _wait_pages(ck_hbm, kbuf, slot, ksem)
    mrun = jnp.full((rows, LANES), NEG, F32)
    for c in range(nchunk):
        kc = kbuf[slot, c * chunk_pages:(c + 1) * chunk_pages].reshape(cw, KV_WIDTH)
        s = logits(kc) + tile_bias(c * cw, cw)
        s_ref[:, c * cw:(c + 1) * cw] = s
        for j in range(cw // LANES):
            mrun = jnp.maximum(mrun, s[:, j * LANES:(j + 1) * LANES])
    zpad = jnp.zeros((LANES - td, KV_WIDTH), F32)
    s_new = logits(jnp.concatenate([kn_ref[...], zpad], axis=0)) + tile_bias(npages * page, LANES)
    s_ref[:, npages * page:] = s_new
    m = jnp.max(jnp.maximum(mrun, s_new), axis=-1, keepdims=True)

    _wait_pages(cv_hbm, vbuf, slot, vsem)
    lrun = jnp.zeros((rows, LANES), F32)
    acc = jnp.zeros((rows, KV_WIDTH), F32)
    for c in range(nchunk + 1):
        if c < nchunk:
            lo, n = c * cw, cw
            vc = vbuf[slot, c * chunk_pages:(c + 1) * chunk_pages].reshape(cw, KV_WIDTH)
        else:
            lo, n = npages * page, LANES
            vc = jnp.concatenate([vn_ref[...], zpad], axis=0)
        p = jnp.exp(s_ref[:, lo:lo + n] - m)
        for j in range(n // LANES):
            lrun = lrun + p[:, j * LANES:(j + 1) * LANES]
        acc = acc + jnp.dot(p.astype(BF16), vc.astype(BF16), preferred_element_type=F32)
    o = acc / jnp.sum(lrun, axis=-1, keepdims=True)
    for h in range(N_HEADS):
        g = h // GROUP
        a_ref[:, h * HEAD_DIM:(h + 1) * HEAD_DIM] = (
            o[h * td:(h + 1) * td, g * HEAD_DIM:(g + 1) * HEAD_DIM].astype(BF16))


def _sample_attention(page_table, q, bias, k_new, v_new, cache_k, cache_v, td, chunk_pages=4):
    bd, npages = page_table.shape
    page = cache_k.shape[1]
    nk = bias.shape[1]
    rowb = lambda b, pt: (b, 0)
    grid_spec = pltpu.PrefetchScalarGridSpec(
        num_scalar_prefetch=1, grid=(bd,),
        in_specs=[
            pl.BlockSpec((td, ATTN_WIDTH), rowb),
            pl.BlockSpec((td, nk), rowb),
            pl.BlockSpec((td, KV_WIDTH), rowb),
            pl.BlockSpec((td, KV_WIDTH), rowb),
            pl.BlockSpec(memory_space=pl.ANY),
            pl.BlockSpec(memory_space=pl.ANY),
        ],
        out_specs=pl.BlockSpec((td, ATTN_WIDTH), rowb),
        scratch_shapes=[
            pltpu.VMEM((2, npages, page, KV_WIDTH), F32),
            pltpu.VMEM((2, npages, page, KV_WIDTH), F32),
            pltpu.VMEM((N_HEADS * td, nk), F32),
            pltpu.SemaphoreType.DMA((2,)),
            pltpu.SemaphoreType.DMA((2,)),
        ],
    )
    return pl.pallas_call(
        functools.partial(_sample_attn_kernel, chunk_pages=chunk_pages),
        grid_spec=grid_spec, out_shape=jax.ShapeDtypeStruct((bd * td, ATTN_WIDTH), BF16),
        compiler_params=_cparams(("arbitrary",)), name="sample_dsa",
    )(page_table, q, bias, k_new, v_new, cache_k, cache_v)


def _layer_weights(i, g_mix, w_in, g_idx_k, b_idx_k, w_dw, b_dw, g_conv_ln, b_conv_ln, w_out, g_ffn, w_up,
                   w_ffn_conv, b_ffn_conv, w_down, g_ple, w_ple_gate, w_ple, g_final):
    d = w_in.shape[1]
    w = w_in[i]
    n_qkvi = C_KI + IDX_DIM + IDX_HEADS
    pad = jnp.zeros((d, LANES - IDX_DIM - IDX_HEADS), w.dtype)
    w_comb = jnp.concatenate([w[:, :n_qkvi], pad, w[:, n_qkvi:]], axis=1).astype(BF16)
    zpad = jnp.zeros((LANES - IDX_DIM,), F32)
    gik = jnp.concatenate([g_idx_k[i], zpad])[None]
    bik = jnp.concatenate([b_idx_k[i], zpad])[None]
    proj = (g_mix[i][None], w_comb, gik, bik)

    wo = w_out[i].astype(BF16)
    conv = (w_dw[i], b_dw[i][None], g_conv_ln[i][None], b_conv_ln[i][None], wo[:ATTN_WIDTH], wo[ATTN_WIDTH:])

    d_ff = w_down.shape[1]
    nch = d_ff // FFN_CW
    split_cols = lambda m: m.reshape(m.shape[0], nch, FFN_CW).swapaxes(0, 1)
    wu = w_up[i].astype(BF16)
    wf = w_ffn_conv[i]
    bf = b_ffn_conv[i][None]
    ffn = (g_ffn[i][None], split_cols(wu[:, :d_ff]), split_cols(wu[:, d_ff:]),
           split_cols(wf[:, :d_ff]), split_cols(wf[:, d_ff:]), split_cols(bf[:, :d_ff]), split_cols(bf[:, d_ff:]),
           w_down[i].astype(BF16).reshape(nch, FFN_CW, d), g_ple[i][None], w_ple_gate[i].astype(BF16),
           w_ple[i].astype(BF16), g_final[None])
    return proj, conv, ffn


def _ffn_state_in(state, nch):
    rows = state.shape[0]
    return state.reshape(rows, 2 * nch, FFN_CW).swapaxes(0, 1)


def _ffn_state_out(st):
    n2, rows, cw = st.shape
    return st.swapaxes(0, 1).reshape(rows, n2 * cw)


def kernel(x_prompt, x_sample, p_prompt, p_sample, cache_k, cache_v, cache_idx_k, state_conv, state_ffn_conv,
           page_table, g_mix, w_in, g_idx_k, b_idx_k, w_dw, b_dw, g_conv_ln, b_conv_ln, w_out, g_ffn, w_up,
           w_ffn_conv, b_ffn_conv, w_down, g_ple, w_ple_gate, w_ple, g_final):
    B, S, D = x_prompt.shape
    Bd, Td, _ = x_sample.shape
    depth = w_in.shape[0]
    n_pages = page_table.shape[1]
    page = cache_k.shape[2]
    past_len = n_pages * page
    d_ff = w_down.shape[1]
    nch = d_ff // FFN_CW
    topk_prompt = min(TOPK_MAX, S // 4)
    topk_sample = min(TOPK_MAX, (past_len + Td) // 4)
    assert S % LANES == 0 and Td == SUBLANES and d_ff % FFN_CW == 0

    tm_p = min(512, S)
    BL = min(32, Bd)
    G = Bd // BL
    tm_s = Td * BL

    tabs_p = _rope_tables(jnp.arange(S, dtype=I32))
    pos_s = past_len + jnp.arange(Td, dtype=I32)
    tabs_s = tuple(jnp.tile(t, (Bd, 1)) for t in _rope_tables(pos_s))

    def to_tm(a):
        return a.reshape(G, BL, Td, a.shape[-1]).transpose(0, 2, 1, 3).reshape(G * tm_s, a.shape[-1])

    def from_tm(a):
        return a.reshape(G, Td, BL, a.shape[-1]).transpose(0, 2, 1, 3).reshape(Bd, Td, a.shape[-1])

    hp = x_prompt.reshape(B * S, D)
    hs = x_sample.reshape(Bd * Td, D)
    outs = {k: [] for k in ("kp", "vp", "ip", "cp", "fp", "ks", "vs", "is", "cs", "fs")}
    for i in range(depth):
        proj_w, conv_w, ffn_w = _layer_weights(
            i, g_mix, w_in, g_idx_k, b_idx_k, w_dw, b_dw, g_conv_ln, b_conv_ln, w_out, g_ffn, w_up,
            w_ffn_conv, b_ffn_conv, w_down, g_ple, w_ple_gate, w_ple, g_final)

        q, k, v, kb, vb, qi, ki, kib, _, wit, glu = _project(hp, tabs_p, S // tm_p, *proj_w, tm=tm_p)
        a = _prompt_attention(q, qi, wit, kib, kb, vb, B, S, topk_prompt)
        nt = S // tm_p
        halo_rows = 32
        halo_spec = pl.BlockSpec(
            (halo_rows, CONV_CH), lambda b, t: (jnp.maximum((b * nt + t) * (tm_p // halo_rows) - 1, 0), 0))
        h1 = _conv_out(glu, glu, halo_spec, a, hp, *conv_w, groups=B, nt=nt, tm=tm_p, shift=1, zero_first=True)
        st0 = jnp.zeros((B, 2 * nch, SUBLANES, FFN_CW), F32)
        hp, st = _ffn(h1, p_prompt[i].reshape(B * S, -1), st0, ffn_w, groups=B, nt=nt, tm=tm_p,
                      shift=1, halo=SUBLANES, final=(i == depth - 1))
        outs["kp"].append(k.reshape(B, S, N_KV_HEADS, HEAD_DIM))
        outs["vp"].append(v.reshape(B, S, N_KV_HEADS, HEAD_DIM))
        outs["ip"].append(ki.reshape(B, S, IDX_DIM))
        outs["cp"].append(glu.reshape(B, S, CONV_CH)[:, S - (CONV_W - 1):])
        outs["fp"].append(jax.vmap(_ffn_state_out)(st)[:, SUBLANES - (FFN_CONV_W - 1):])

        tm_sp = min(512, Bd * Td)
        q, k, v, _, _, qi, ki, kib, wi, _, glu = _project(hs, tabs_s, Bd * Td // tm_sp, *proj_w, tm=tm_sp)
        keys = _sample_scores(page_table, qi, wi, kib, cache_idx_k[i], Td)
        bias = _select(keys, topk_sample, rows=min(64, Bd * Td))
        a = _sample_attention(page_table, q, bias, k, v,
                              cache_k[i].reshape(-1, page, KV_WIDTH), cache_v[i].reshape(-1, page, KV_WIDTH), Td)
        sc = state_conv[i]
        halo_s = sc.reshape(G, BL, CONV_W - 1, CONV_CH).transpose(0, 2, 1, 3).reshape(G * (CONV_W - 1) * BL, CONV_CH)
        halo_spec = pl.BlockSpec(((CONV_W - 1) * BL, CONV_CH), lambda g, t: (g, 0))
        h1 = _conv_out(to_tm(glu), halo_s, halo_spec, to_tm(a), to_tm(hs), *conv_w, groups=G, nt=1, tm=tm_s,
                       shift=BL, zero_first=False)
        sf = state_ffn_conv[i]
        sf = sf.reshape(G, BL, FFN_CONV_W - 1, 2 * d_ff).transpose(0, 2, 1, 3).reshape(G, (FFN_CONV_W - 1) * BL, 2 * d_ff)
        st0 = jax.vmap(lambda s: _ffn_state_in(s, nch))(sf)
        hs_tm, st = _ffn(h1, to_tm(p_sample[i].reshape(Bd * Td, -1)), st0, ffn_w, groups=G, nt=1, tm=tm_s,
                         shift=BL, halo=(FFN_CONV_W - 1) * BL, final=(i == depth - 1))
        hs = from_tm(hs_tm).reshape(Bd * Td, D)
        fs = jax.vmap(_ffn_state_out)(st)
        fs = fs.reshape(G, FFN_CONV_W - 1, BL, 2 * d_ff).transpose(0, 2, 1, 3).reshape(Bd, FFN_CONV_W - 1, 2 * d_ff)
        outs["ks"].append(k.reshape(Bd, Td, N_KV_HEADS, HEAD_DIM))
        outs["vs"].append(v.reshape(Bd, Td, N_KV_HEADS, HEAD_DIM))
        outs["is"].append(ki.reshape(Bd, Td, IDX_DIM))
        outs["cs"].append(jnp.concatenate([sc, glu.reshape(Bd, Td, CONV_CH)], axis=1)[:, Td:])
        outs["fs"].append(fs)

    y_prompt = hp.reshape(B, S, D)
    y_sample = hs.reshape(Bd, Td, D)
    st = lambda name: jnp.stack(outs[name])
    return (y_prompt, y_sample, st("kp"), st("vp"), st("ip"), st("cp"), st("fp"),
            st("ks"), st("vs"), st("is"), st("cs"), st("fs"))
```

```python
import functools

import jax
import jax.numpy as jnp
from jax import lax
from jax.experimental import pallas as pl
from jax.experimental.pallas import tpu as pltpu

N_HEADS = 8
HEAD_DIM = 64
N_KV_HEADS = 2
GROUP = N_HEADS // N_KV_HEADS
ATTN_WIDTH = N_HEADS * HEAD_DIM
KV_WIDTH = N_KV_HEADS * HEAD_DIM
IDX_HEADS = 8
IDX_DIM = 64
IDX_ROPE_DIM = 32
TOPK_MAX = 256
CONV_W = 31
FFN_CONV_W = 3
ROPE_THETA = 10000.0
EPS = 1e-6
NEG = -1e30

LANES = 128
SUBLANES = 8
INT_MIN = -2 ** 31
VMEM_LIMIT = 56 * 1024 * 1024

TQ = LANES
TPC = 4
KC = TPC * LANES

F32 = jnp.float32
BF16 = jnp.bfloat16
I32 = jnp.int32

_NT = (((1,), (1,)), ((), ()))


def _cparams(sem):
    return pltpu.CompilerParams(dimension_semantics=sem, vmem_limit_bytes=VMEM_LIMIT)


def _sortable(x):
    b = pltpu.bitcast(x, I32)
    return b ^ ((b >> 31) & jnp.int32(0x7FFFFFFF))


def _rmsnorm(x, g):
    return x * lax.rsqrt(jnp.mean(x * x, axis=-1, keepdims=True) + EPS) * g


C_Q = 0
C_K = C_Q + ATTN_WIDTH
C_V = C_K + KV_WIDTH
C_QI = C_V + KV_WIDTH
C_KI = C_QI + IDX_HEADS * IDX_DIM
C_A = C_KI + LANES
CONV_CH = 512
C_G = C_A + CONV_CH
C_END = C_G + CONV_CH


def _proj_kernel(x_ref, g_ref, w_ref, gik_ref, bik_ref, cq_ref, sq_ref, ci_ref, si_ref, *out_refs,
                 wi_scale, prompt):
    tm = x_ref.shape[0]
    hn = _rmsnorm(x_ref[...], g_ref[...]).astype(BF16)
    z = jnp.dot(hn, w_ref[...], preferred_element_type=F32)

    lane = lax.broadcasted_iota(I32, (tm, LANES), 1)
    in_head = lane % HEAD_DIM
    cq, sq, ci, si = cq_ref[...], sq_ref[...], ci_ref[...], si_ref[...]

    def rope_full(xg):
        sw = jnp.where(in_head < HEAD_DIM // 2, pltpu.roll(xg, LANES - HEAD_DIM // 2, 1),
                       pltpu.roll(xg, HEAD_DIM // 2, 1))
        return xg * cq + sw * sq

    def rope_part(xg):
        sw = jnp.where(in_head < IDX_ROPE_DIM // 2, pltpu.roll(xg, LANES - IDX_ROPE_DIM // 2, 1),
                       pltpu.roll(xg, IDX_ROPE_DIM // 2, 1))
        return xg * ci + sw * si

    q_groups = [rope_full(z[:, C_Q + g * LANES:C_Q + (g + 1) * LANES]) * (HEAD_DIM ** -0.5)
                for g in range(ATTN_WIDTH // LANES)]
    qi_groups = [rope_part(z[:, C_QI + g * LANES:C_QI + (g + 1) * LANES])
                 for g in range(IDX_HEADS * IDX_DIM // LANES)]
    kr = rope_full(z[:, C_K:C_K + LANES])
    vr = z[:, C_V:C_V + LANES]

    zg = z[:, C_KI:C_KI + LANES]
    is_ki = lane < IDX_DIM
    mu = jnp.sum(jnp.where(is_ki, zg, 0.0), axis=-1, keepdims=True) / IDX_DIM
    xc = jnp.where(is_ki, zg - mu, 0.0)
    var = jnp.sum(xc * xc, axis=-1, keepdims=True) / IDX_DIM
    kin = rope_part(xc * lax.rsqrt(var + EPS) * gik_ref[...] + bik_ref[...])
    wig = zg * wi_scale
    glu = z[:, C_A:C_A + CONV_CH] * jax.nn.sigmoid(z[:, C_G:C_G + CONV_CH])

    if prompt:
        q_ref, qit_ref, kt_ref, vt_ref, kit_ref, ktb_ref, vb_ref, kib_ref, wit_ref, glu_ref = out_refs
        for g, qg in enumerate(q_groups):
            q_ref[:, g * LANES:(g + 1) * LANES] = qg.astype(BF16)
        for g, qig in enumerate(qi_groups):
            qit_ref[g * LANES:(g + 1) * LANES, :] = qig.T.astype(BF16)
        krt = kr.T
        vrt = vr.T
        for g in range(N_KV_HEADS):
            rows = slice(g * HEAD_DIM, (g + 1) * HEAD_DIM)
            kt_ref[0, g] = krt[rows]
            vt_ref[0, g] = vrt[rows]
            ktb_ref[0, g] = krt[rows].astype(BF16)
            vb_ref[g] = vr[:, rows].astype(BF16)
        kit_ref[0] = kin.T[:IDX_DIM]
        kib_ref[...] = kin[:, :IDX_DIM].astype(BF16)
        wit_ref[...] = wig.T[IDX_DIM:IDX_DIM + IDX_HEADS, :]
        glu_ref[...] = glu
    else:
        q_ref, qi_ref, k_ref, v_ref, ki_ref, kib_ref, wi_ref, glu_ref = out_refs
        for g, qg in enumerate(q_groups):
            q_ref[:, g * LANES:(g + 1) * LANES] = qg.astype(BF16)
        for g, qig in enumerate(qi_groups):
            qi_ref[:, g * LANES:(g + 1) * LANES] = qig.astype(BF16)
        k_ref[...] = kr
        v_ref[...] = vr
        ki_ref[...] = kin[:, :IDX_DIM]
        kib_ref[...] = kin[:, :IDX_DIM].astype(BF16)
        wi_ref[...] = wig[:, IDX_DIM:IDX_DIM + IDX_HEADS]
        glu_ref[...] = glu


def _rope_tables(pos):
    def cs(half):
        inv = jnp.power(jnp.float32(ROPE_THETA), -jnp.arange(half, dtype=F32) / half)
        ang = pos.astype(F32)[:, None] * inv[None, :]
        return jnp.cos(ang), jnp.sin(ang)

    c, s = cs(HEAD_DIM // 2)
    cq = jnp.concatenate([c, c, c, c], axis=-1)
    sq = jnp.concatenate([-s, s, -s, s], axis=-1)
    c2, s2 = cs(IDX_ROPE_DIM // 2)
    one = jnp.ones((pos.shape[0], IDX_DIM - IDX_ROPE_DIM), F32)
    ci = jnp.concatenate([c2, c2, one, c2, c2, one], axis=-1)
    si = jnp.concatenate([-s2, s2, 0 * one, -s2, s2, 0 * one], axis=-1)
    return cq, sq, ci, si


def _project(x, tables, tab_blocks, g_mix, w_comb, gik, bik, *, tm, prompt, batch=None):
    n, d = x.shape
    nb = n // tm
    row = lambda i: (i, 0)
    const = lambda i: (0, 0)
    tab = lambda i: (i % tab_blocks, 0)
    wi_scale = IDX_HEADS ** -0.5 * IDX_DIM ** -0.5
    sds = jax.ShapeDtypeStruct
    if prompt:
        seq = n // batch
        nt = seq // tm
        bt = lambda i: (i // nt, 0, 0, i % nt)
        out_shape = (
            sds((n, ATTN_WIDTH), BF16),
            sds((IDX_HEADS * IDX_DIM, n), BF16),
            sds((batch, N_KV_HEADS, HEAD_DIM, seq), F32),
            sds((batch, N_KV_HEADS, HEAD_DIM, seq), F32),
            sds((batch, IDX_DIM, seq), F32),
            sds((nb, N_KV_HEADS, HEAD_DIM, tm), BF16),
            sds((N_KV_HEADS, n, HEAD_DIM), BF16),
            sds((n, IDX_DIM), BF16),
            sds((IDX_HEADS, n), F32),
            sds((n, CONV_CH), F32),
        )
        out_specs = (
            pl.BlockSpec((tm, ATTN_WIDTH), row),
            pl.BlockSpec((IDX_HEADS * IDX_DIM, tm), lambda i: (0, i)),
            pl.BlockSpec((1, N_KV_HEADS, HEAD_DIM, tm), bt),
            pl.BlockSpec((1, N_KV_HEADS, HEAD_DIM, tm), bt),
            pl.BlockSpec((1, IDX_DIM, tm), lambda i: (i // nt, 0, i % nt)),
            pl.BlockSpec((1, N_KV_HEADS, HEAD_DIM, tm), lambda i: (i, 0, 0, 0)),
            pl.BlockSpec((N_KV_HEADS, tm, HEAD_DIM), lambda i: (0, i, 0)),
            pl.BlockSpec((tm, IDX_DIM), row),
            pl.BlockSpec((IDX_HEADS, tm), lambda i: (0, i)),
            pl.BlockSpec((tm, CONV_CH), row),
        )
    else:
        out_shape = (
            sds((n, ATTN_WIDTH), BF16),
            sds((n, IDX_HEADS * IDX_DIM), BF16),
            sds((n, KV_WIDTH), F32),
            sds((n, KV_WIDTH), F32),
            sds((n, IDX_DIM), F32),
            sds((n, IDX_DIM), BF16),
            sds((n, IDX_HEADS), F32),
            sds((n, CONV_CH), F32),
        )
        out_specs = (
            pl.BlockSpec((tm, ATTN_WIDTH), row),
            pl.BlockSpec((tm, IDX_HEADS * IDX_DIM), row),
            pl.BlockSpec((tm, KV_WIDTH), row),
            pl.BlockSpec((tm, KV_WIDTH), row),
            pl.BlockSpec((tm, IDX_DIM), row),
            pl.BlockSpec((tm, IDX_DIM), row),
            pl.BlockSpec((tm, IDX_HEADS), row),
            pl.BlockSpec((tm, CONV_CH), row),
        )
    in_specs = [
        pl.BlockSpec((tm, d), row),
        pl.BlockSpec((1, d), const),
        pl.BlockSpec((d, C_END), const),
        pl.BlockSpec((1, LANES), const),
        pl.BlockSpec((1, LANES), const),
    ] + [pl.BlockSpec((tm, LANES), tab)] * 4
    return pl.pallas_call(
        functools.partial(_proj_kernel, wi_scale=wi_scale, prompt=prompt),
        grid=(nb,), in_specs=in_specs, out_specs=out_specs, out_shape=out_shape,
        compiler_params=_cparams(("parallel",)), name="in_proj",
    )(x, g_mix, w_comb, gik, bik, *tables)


def _tile_count(pred):
    return jnp.sum(pred.astype(I32).reshape(LANES // SUBLANES, SUBLANES, LANES), axis=0)


def _count_keys(key_ref, nch, pred):
    def body(c, accs):
        return tuple(a + _tile_count(pred(key_ref[c * TPC + j], c * TPC + j)) for j, a in enumerate(accs))
    z8 = jnp.zeros((SUBLANES, LANES), I32)
    accs = lax.fori_loop(0, nch, body, (z8,) * TPC)
    return jnp.sum(sum(accs[1:], accs[0]), axis=0, keepdims=True)


def _prompt_attn_kernel(qit_ref, wit_ref, q_ref, kib_ref, ktb_ref, vb_ref, a_ref,
                        key_ref, bias_ref, s_ref, m_ref, l_ref, acc_ref, *, topk):
    i = pl.program_id(1)
    nch = i // TPC + 1
    r_io = lax.broadcasted_iota(I32, (TQ, TQ), 0)
    c_io = lax.broadcasted_iota(I32, (TQ, TQ), 1)

    def causal(kt):
        return r_io + (kt - i) * TQ <= c_io

    qit = jnp.concatenate([qit_ref[h * IDX_DIM:(h + 1) * IDX_DIM, :] for h in range(IDX_HEADS)], axis=1)
    w_rows = [wit_ref[h:h + 1, :] for h in range(IDX_HEADS)]

    def score_body(c, carry):
        for j in range(TPC):
            kt = c * TPC + j
            kit = kib_ref[pl.ds(pl.multiple_of(kt * TQ, TQ), TQ), :]
            lg = jnp.dot(kit, qit, preferred_element_type=F32)
            sc = jnp.maximum(lg[:, :TQ], 0.0) * w_rows[0]
            for h in range(1, IDX_HEADS):
                sc = sc + jnp.maximum(lg[:, h * TQ:(h + 1) * TQ], 0.0) * w_rows[h]
            key_ref[kt] = jnp.where(causal(kt), _sortable(sc), INT_MIN)
        return carry
    lax.fori_loop(0, nch, score_body, 0)

    n_ge0 = _count_keys(key_ref, nch, lambda k, kt: k >= 0)
    t0 = jnp.where(n_ge0 >= topk, 0, INT_MIN).astype(I32)

    def bit_body(p, t):
        cand = t | jnp.left_shift(jnp.int32(1), 30 - p)
        return jnp.where(_count_keys(key_ref, nch, lambda k, kt: k >= cand) >= topk, cand, t)
    thr = lax.fori_loop(0, 31, bit_body, t0)

    n_gt = _count_keys(key_ref, nch, lambda k, kt: k > thr)
    n_eq = _count_keys(key_ref, nch, lambda k, kt: k == thr)
    need = topk - n_gt
    n_idx_bits = max(1, (key_ref.shape[0] * TQ).bit_length())

    def tie_search(_):
        def body(p, m):
            cand = m | jnp.left_shift(jnp.int32(1), n_idx_bits - 1 - p)
            below = _count_keys(key_ref, nch, lambda k, kt: (k == thr) & (r_io + kt * TQ < cand))
            return jnp.where(below < need, cand, m)
        return lax.fori_loop(0, n_idx_bits, body, jnp.zeros((1, TQ), I32))

    excess = jnp.max(jnp.where((need > 0) & (thr > INT_MIN), n_eq - need, 0)) > 0
    idx_max = lax.cond(excess, tie_search,
                       lambda _: jnp.full((1, TQ), key_ref.shape[0] * TQ, I32), 0)

    def bias_body(c, carry):
        for j in range(TPC):
            kt = c * TPC + j
            k = key_ref[kt]
            sel = ((k > thr) | ((k == thr) & (r_io + kt * TQ <= idx_max))) & causal(kt)
            bias_ref[kt] = jnp.where(sel, 0.0, NEG).astype(F32).T
        return carry
    lax.fori_loop(0, nch, bias_body, 0)

    q = q_ref[...]
    for g in range(N_KV_HEADS):
        q4 = jnp.concatenate([q[:, (GROUP * g + hh) * HEAD_DIM:(GROUP * g + hh + 1) * HEAD_DIM]
                              for hh in range(GROUP)], axis=0)
        m_ref[...] = jnp.full(m_ref.shape, NEG, F32)

        def qk_body(c, carry):
            s = jnp.dot(q4, ktb_ref[c, g], preferred_element_type=F32)
            bias = jnp.concatenate([bias_ref[c * TPC + j] for j in range(TPC)], axis=1)
            s = (s.reshape(GROUP, TQ, KC) + bias[None]).reshape(GROUP * TQ, KC)
            s_ref[c] = s
            cm = s[:, :TQ]
            for j in range(1, TPC):
                cm = jnp.maximum(cm, s[:, j * TQ:(j + 1) * TQ])
            m_ref[...] = jnp.maximum(m_ref[...], cm)
            return carry
        lax.fori_loop(0, nch, qk_body, 0)
        m = jnp.max(m_ref[...], axis=-1, keepdims=True)
        l_ref[...] = jnp.zeros(l_ref.shape, F32)
        acc_ref[...] = jnp.zeros(acc_ref.shape, F32)

        def pv_body(c, carry):
            p = jnp.exp(s_ref[c] - m)
            ps = p[:, :TQ]
            for j in range(1, TPC):
                ps = ps + p[:, j * TQ:(j + 1) * TQ]
            l_ref[...] += ps
            vc = vb_ref[g, pl.ds(pl.multiple_of(c * KC, KC), KC), :]
            acc_ref[...] += jnp.dot(p.astype(BF16), vc, preferred_element_type=F32)
            return carry
        lax.fori_loop(0, nch, pv_body, 0)
        o = acc_ref[...] / jnp.sum(l_ref[...], axis=-1, keepdims=True)
        for hh in range(GROUP):
            h = GROUP * g + hh
            a_ref[:, h * HEAD_DIM:(h + 1) * HEAD_DIM] = o[hh * TQ:(hh + 1) * TQ, :].astype(BF16)


def _prompt_attention(q, qit, wit, kib, ktb, vb, batch, seq, topk):
    nb = seq // TQ
    ncs = seq // KC
    qrow = lambda b, i: (b * nb + i, 0)
    qcol = lambda b, i: (0, b * nb + i)
    return pl.pallas_call(
        functools.partial(_prompt_attn_kernel, topk=topk),
        grid=(batch, nb),
        in_specs=[
            pl.BlockSpec((IDX_HEADS * IDX_DIM, TQ), qcol),
            pl.BlockSpec((IDX_HEADS, TQ), qcol),
            pl.BlockSpec((TQ, ATTN_WIDTH), qrow),
            pl.BlockSpec((seq, IDX_DIM), lambda b, i: (b, 0)),
            pl.BlockSpec((ncs, N_KV_HEADS, HEAD_DIM, KC), lambda b, i: (b, 0, 0, 0)),
            pl.BlockSpec((N_KV_HEADS, seq, HEAD_DIM), lambda b, i: (0, b, 0)),
        ],
        out_specs=pl.BlockSpec((TQ, ATTN_WIDTH), qrow),
        out_shape=jax.ShapeDtypeStruct((batch * seq, ATTN_WIDTH), BF16),
        scratch_shapes=[
            pltpu.VMEM((nb, TQ, TQ), I32),
            pltpu.VMEM((nb, TQ, TQ), F32),
            pltpu.VMEM((ncs, GROUP * TQ, KC), F32),
            pltpu.VMEM((GROUP * TQ, TQ), F32),
            pltpu.VMEM((GROUP * TQ, TQ), F32),
            pltpu.VMEM((GROUP * TQ, HEAD_DIM), F32),
        ],
        compiler_params=_cparams(("parallel", "arbitrary")), name="prompt_dsa",
    )(qit, wit, q, kib, ktb, vb)


def _conv_out_kernel(glu_ref, halo_ref, a_ref, x_ref, wdw_ref, bdw_ref, gln_ref, bln_ref, wa_ref, wc_ref,
                     h_ref, xp_ref, c_ref, *, shift, zero_first, rc):
    tm = glu_ref.shape[0]
    halo = halo_ref.shape[0]
    hv = halo_ref[...]
    if zero_first:
        hv = jnp.where(pl.program_id(1) == 0, 0.0, hv)
    xp_ref[0:halo, :] = hv
    xp_ref[halo:halo + tm, :] = glu_ref[...]

    for lg in range(CONV_CH // LANES):
        ls = slice(lg * LANES, (lg + 1) * LANES)
        for r0 in range(0, tm, rc):
            acc = jnp.zeros((rc, LANES), F32)
            for j in range(CONV_W):
                off = halo - (CONV_W - 1 - j) * shift
                acc = acc + wdw_ref[j:j + 1, ls] * xp_ref[r0 + off:r0 + off + rc, ls]
            c_ref[r0:r0 + rc, ls] = acc + bdw_ref[:, ls]

    c = c_ref[...]
    mu = jnp.mean(c, axis=-1, keepdims=True)
    xc = c - mu
    var = jnp.mean(xc * xc, axis=-1, keepdims=True)
    y = xc * lax.rsqrt(var + EPS) * gln_ref[...] + bln_ref[...]
    y = jax.nn.silu(y)
    h_ref[...] = (x_ref[...]
                  + jnp.dot(a_ref[...], wa_ref[...], preferred_element_type=F32)
                  + jnp.dot(y.astype(BF16), wc_ref[...], preferred_element_type=F32))


def _conv_out(glu, halo_arr, halo_spec, a, x, w_dw, b_dw, g_ln, b_ln, w_a, w_c, *, groups, nt, tm,
              shift, zero_first):
    d = x.shape[1]
    halo = halo_spec.block_shape[0]
    row = lambda b, i: (b * nt + i, 0)
    const = lambda b, i: (0, 0)
    return pl.pallas_call(
        functools.partial(_conv_out_kernel, shift=shift, zero_first=zero_first, rc=32),
        grid=(groups, nt),
        in_specs=[
            pl.BlockSpec((tm, CONV_CH), row),
            halo_spec,
            pl.BlockSpec((tm, ATTN_WIDTH), row),
            pl.BlockSpec((tm, d), row),
            pl.BlockSpec((CONV_W, CONV_CH), const),
            pl.BlockSpec((1, CONV_CH), const),
            pl.BlockSpec((1, CONV_CH), const),
            pl.BlockSpec((1, CONV_CH), const),
            pl.BlockSpec((ATTN_WIDTH, d), const),
            pl.BlockSpec((CONV_CH, d), const),
        ],
        out_specs=pl.BlockSpec((tm, d), row),
        out_shape=jax.ShapeDtypeStruct(x.shape, F32),
        scratch_shapes=[pltpu.VMEM((halo + tm, CONV_CH), F32), pltpu.VMEM((tm, CONV_CH), F32)],
        compiler_params=_cparams(("parallel", "arbitrary")), name="conv_out_proj",
    )(glu, halo_arr, a, x, w_dw, b_dw, g_ln, b_ln, w_a, w_c)


FFN_CW = 256
FFN_RC = 64


def _ffn_kernel(h_ref, p_ref, st_ref, gffn_ref, wug_ref, wuv_ref, wfg_ref, wfv_ref, bfg_ref, bfv_ref,
                wdn_ref, gple_ref, wgate_ref, wple_ref, gfin_ref,
                y_ref, sto_ref, u_ref, act_ref, carry_ref, acc_ref, *, shift, halo, final):
    tm = h_ref.shape[0]
    nch = wug_ref.shape[0]
    first = pl.program_id(1) == 0
    h = h_ref[...]
    hn = _rmsnorm(h, gffn_ref[...]).astype(BF16)
    acc_ref[...] = jnp.zeros(acc_ref.shape, F32)

    def chunk(c, buf):
        halves = ((wug_ref, wfg_ref, bfg_ref, c), (wuv_ref, wfv_ref, bfv_ref, nch + c))
        for hf, (wu_ref, _, _, slot) in enumerate(halves):
            ub = u_ref.at[buf, hf]

            @pl.when(first)
            def _():
                ub[0:halo, :] = st_ref[slot]

            @pl.when(jnp.logical_not(first))
            def _():
                ub[0:halo, :] = carry_ref[slot]
            ub[halo:halo + tm, :] = jnp.dot(hn, wu_ref[c], preferred_element_type=F32)
            tail = ub[tm:tm + halo, :]
            carry_ref[slot] = tail
            sto_ref[slot] = tail
        for r0 in range(0, tm, FFN_RC):
            conv = []
            for hf, (_, wf_ref, bf_ref, _) in enumerate(halves):
                wf = wf_ref[c]
                out = bf_ref[c]
                for k in range(FFN_CONV_W):
                    off = r0 + halo - (FFN_CONV_W - 1 - k) * shift
                    out = out + wf[k:k + 1, :] * u_ref[buf, hf, off:off + FFN_RC, :]
                conv.append(out)
            act_ref[buf, r0:r0 + FFN_RC, :] = (jax.nn.silu(conv[0]) * conv[1]).astype(BF16)
        acc_ref[...] += jnp.dot(act_ref[buf], wdn_ref[c], preferred_element_type=F32)

    def pair(k, carry):
        chunk(2 * k, 0)
        chunk(2 * k + 1, 1)
        return carry
    lax.fori_loop(0, nch // 2, pair, 0)
    if nch % 2:
        chunk(nch - 1, 0)

    h2 = h + acc_ref[...]
    gate = jax.nn.sigmoid(jnp.dot(_rmsnorm(h2, gple_ref[...]).astype(BF16), wgate_ref[...],
                                  preferred_element_type=F32))
    ple = jnp.dot(p_ref[...].astype(BF16), wple_ref[...], preferred_element_type=F32)
    h3 = h2 + ple * gate
    y_ref[...] = _rmsnorm(h3, gfin_ref[...]) if final else h3


def _ffn(h, p, st, wts, *, groups, nt, tm, shift, halo, final):
    (g_ffn, wug, wuv, wfg, wfv, bfg, bfv, wdn, g_ple, w_gate, w_ple, g_fin) = wts
    d = h.shape[1]
    nch, _, cw = wug.shape
    row = lambda b, i: (b * nt + i, 0)
    c2 = lambda b, i: (0, 0)
    c3 = lambda b, i: (0, 0, 0)
    stspec = pl.BlockSpec((None, 2 * nch, halo, cw), lambda b, i: (b, 0, 0, 0))
    once = dict(pipeline_mode=pl.Buffered(1))
    return pl.pallas_call(
        functools.partial(_ffn_kernel, shift=shift, halo=halo, final=final),
        grid=(groups, nt),
        in_specs=[
            pl.BlockSpec((tm, d), row),
            pl.BlockSpec((tm, p.shape[1]), row),
            stspec,
            pl.BlockSpec((1, d), c2),
            pl.BlockSpec(wug.shape, c3, **once),
            pl.BlockSpec(wuv.shape, c3, **once),
            pl.BlockSpec(wfg.shape, c3),
            pl.BlockSpec(wfv.shape, c3),
            pl.BlockSpec(bfg.shape, c3),
            pl.BlockSpec(bfv.shape, c3),
            pl.BlockSpec(wdn.shape, c3, **once),
            pl.BlockSpec((1, d), c2),
            pl.BlockSpec(w_gate.shape, c2, **once),
            pl.BlockSpec(w_ple.shape, c2, **once),
            pl.BlockSpec((1, d), c2),
        ],
        out_specs=(pl.BlockSpec((tm, d), row), stspec),
        out_shape=(jax.ShapeDtypeStruct(h.shape, F32),
                   jax.ShapeDtypeStruct((groups, 2 * nch, halo, cw), F32)),
        scratch_shapes=[
            pltpu.VMEM((2, 2, halo + tm, cw), F32),
            pltpu.VMEM((2, tm, cw), BF16),
            pltpu.VMEM((2 * nch, halo, cw), F32),
            pltpu.VMEM((tm, d), F32),
        ],
        compiler_params=_cparams(("parallel", "arbitrary")), name="conv_ffn_ple",
    )(h, p, st, g_ffn, wug, wuv, wfg, wfv, bfg, bfv, wdn, g_ple, w_gate, w_ple, g_fin)


def _start_pages(pt_ref, b, src_hbm, dst_ref, slot, sem):
    for p in range(dst_ref.shape[1]):
        pltpu.make_async_copy(src_hbm.at[pt_ref[b, p]], dst_ref.at[slot, p], sem.at[slot]).start()


def _wait_pages(src_hbm, dst_ref, slot, sem):
    for p in range(dst_ref.shape[1]):
        pltpu.make_async_copy(src_hbm.at[0], dst_ref.at[slot, p], sem.at[slot]).wait()


def _rows_by_head(x, width):
    return jnp.concatenate([x[:, h * width:(h + 1) * width] for h in range(x.shape[1] // width)], axis=0)


def _pad_rows(x, rows):
    return jnp.concatenate([x, jnp.zeros((rows - x.shape[0], x.shape[1]), x.dtype)], axis=0)


def _sample_score_kernel(pt_ref, qi_ref, wi_ref, kin_ref, cidx_hbm, key_ref, ibuf, sem, *, chunk_pages):
    b = pl.program_id(0)
    nb = pl.num_programs(0)
    slot = b % 2
    npages, page = ibuf.shape[1], ibuf.shape[3]
    td = qi_ref.shape[0]

    @pl.when(b == 0)
    def _():
        _start_pages(pt_ref, b, cidx_hbm, ibuf, 0, sem)

    @pl.when(b + 1 < nb)
    def _():
        _start_pages(pt_ref, b + 1, cidx_hbm, ibuf, 1 - slot, sem)

    qi = _rows_by_head(qi_ref[...], IDX_DIM)
    wcol = jnp.concatenate([wi_ref[:, h:h + 1] for h in range(IDX_HEADS)], axis=0)

    def head_sum(lg):
        r = jnp.maximum(lg, 0.0) * wcol
        return jnp.sum(r.reshape(IDX_HEADS, td, r.shape[1]), axis=0)

    _wait_pages(cidx_hbm, ibuf, slot, sem)
    cw = chunk_pages * page
    for c in range(npages // chunk_pages):
        kt = jnp.concatenate([ibuf[slot, c * chunk_pages + j] for j in range(chunk_pages)], axis=1)
        lg = jnp.dot(qi, kt.astype(BF16), preferred_element_type=F32)
        key_ref[:, c * cw:(c + 1) * cw] = _sortable(head_sum(lg))

    lg = lax.dot_general(qi, _pad_rows(kin_ref[...], LANES), _NT, preferred_element_type=F32)
    sc = head_sum(lg)
    t_io = lax.broadcasted_iota(I32, sc.shape, 0)
    j_io = lax.broadcasted_iota(I32, sc.shape, 1)
    key_ref[:, npages * page:] = jnp.where(j_io <= t_io, _sortable(sc), INT_MIN)


def _sample_scores(page_table, qi, wi, kib, cache_idx_t, td, chunk_pages=4):
    bd, npages = page_table.shape
    page = cache_idx_t.shape[2]
    nk = npages * page + LANES
    grid_spec = pltpu.PrefetchScalarGridSpec(
        num_scalar_prefetch=1, grid=(bd,),
        in_specs=[
            pl.BlockSpec((td, IDX_HEADS * IDX_DIM), lambda b, pt: (b, 0)),
            pl.BlockSpec((td, IDX_HEADS), lambda b, pt: (b, 0)),
            pl.BlockSpec((td, IDX_DIM), lambda b, pt: (b, 0)),
            pl.BlockSpec(memory_space=pl.ANY),
        ],
        out_specs=pl.BlockSpec((td, nk), lambda b, pt: (b, 0)),
        scratch_shapes=[pltpu.VMEM((2, npages, IDX_DIM, page), F32), pltpu.SemaphoreType.DMA((2,))],
    )
    return pl.pallas_call(
        functools.partial(_sample_score_kernel, chunk_pages=chunk_pages),
        grid_spec=grid_spec, out_shape=jax.ShapeDtypeStruct((bd * td, nk), I32),
        compiler_params=_cparams(("arbitrary",)), name="sample_scores",
    )(page_table, qi, wi, kib, cache_idx_t)


def _select_kernel(key_ref, bias_ref, *, topk):
    rows, nk = key_ref.shape
    ntile = nk // LANES

    def count(pred):
        acc = jnp.zeros((rows, LANES), I32)
        for c in range(ntile):
            acc = acc + pred(key_ref[:, c * LANES:(c + 1) * LANES], c).astype(I32)
        return jnp.sum(acc, axis=-1, keepdims=True)

    t0 = jnp.where(count(lambda k, c: k >= 0) >= topk, 0, INT_MIN).astype(I32)

    def bit_body(p, t):
        cand = t | jnp.left_shift(jnp.int32(1), 30 - p)
        return jnp.where(count(lambda k, c: k >= cand) >= topk, cand, t)
    thr = lax.fori_loop(0, 31, bit_body, t0)

    n_gt = count(lambda k, c: k > thr)
    n_eq = count(lambda k, c: k == thr)
    need = topk - n_gt
    l_io = lax.broadcasted_iota(I32, (rows, LANES), 1)
    n_idx_bits = max(1, nk.bit_length())

    def tie_search(_):
        def body(p, m):
            cand = m | jnp.left_shift(jnp.int32(1), n_idx_bits - 1 - p)
            below = count(lambda k, c: (k == thr) & (l_io + c * LANES < cand))
            return jnp.where(below < need, cand, m)
        return lax.fori_loop(0, n_idx_bits, body, jnp.zeros((rows, 1), I32))

    excess = jnp.max(jnp.where((need > 0) & (thr > INT_MIN), n_eq - need, 0)) > 0
    idx_max = lax.cond(excess, tie_search, lambda _: jnp.full((rows, 1), nk, I32), 0)

    for c in range(ntile):
        k = key_ref[:, c * LANES:(c + 1) * LANES]
        sel = ((k > thr) | ((k == thr) & (l_io + c * LANES <= idx_max))) & (k != INT_MIN)
        bias_ref[:, c * LANES:(c + 1) * LANES] = jnp.where(sel, 0.0, NEG).astype(F32)


def _select(keys, topk, rows):
    n, nk = keys.shape
    return pl.pallas_call(
        functools.partial(_select_kernel, topk=topk),
        grid=(n // rows,),
        in_specs=[pl.BlockSpec((rows, nk), lambda i: (i, 0))],
        out_specs=pl.BlockSpec((rows, nk), lambda i: (i, 0)),
        out_shape=jax.ShapeDtypeStruct((n, nk), F32),
        compiler_params=_cparams(("parallel",)), name="sample_select",
    )(keys)


def _sample_attn_kernel(pt_ref, q_ref, bias_ref, kn_ref, vn_ref, ck_hbm, cv_hbm, a_ref,
                        kbuf, vbuf, s_ref, ksem, vsem, *, chunk_pages):
    b = pl.program_id(0)
    nb = pl.num_programs(0)
    slot = b % 2
    npages, page = kbuf.shape[1], kbuf.shape[4]
    td = q_ref.shape[0]
    half = GROUP * td

    @pl.when(b == 0)
    def _():
        _start_pages(pt_ref, b, ck_hbm, kbuf, 0, ksem)
        _start_pages(pt_ref, b, cv_hbm, vbuf, 0, vsem)

    @pl.when(b + 1 < nb)
    def _():
        _start_pages(pt_ref, b + 1, ck_hbm, kbuf, 1 - slot, ksem)
        _start_pages(pt_ref, b + 1, cv_hbm, vbuf, 1 - slot, vsem)

    q = _rows_by_head(q_ref[...], HEAD_DIM)
    qg = [q[g * half:(g + 1) * half] for g in range(N_KV_HEADS)]
    cw = chunk_pages * page
    nchunk = npages // chunk_pages
    new_lo = npages * page
    kn = _pad_rows(kn_ref[...], LANES).astype(BF16)
    vn = _pad_rows(vn_ref[...], LANES).astype(BF16)

    def chunk_t(buf, c, g):
        return jnp.concatenate([buf[slot, c * chunk_pages + j, g] for j in range(chunk_pages)],
                               axis=1).astype(BF16)

    def tile_bias(lo, n):
        return jnp.concatenate([bias_ref[:, lo:lo + n]] * GROUP, axis=0)

    _wait_pages(ck_hbm, kbuf, slot, ksem)
    ms = []
    for g in range(N_KV_HEADS):
        rows = slice(g * half, (g + 1) * half)
        mrun = jnp.full((half, LANES), NEG, F32)
        for c in range(nchunk):
            s = jnp.dot(qg[g], chunk_t(kbuf, c, g), preferred_element_type=F32) + tile_bias(c * cw, cw)
            s_ref[rows, c * cw:(c + 1) * cw] = s
            for j in range(cw // LANES):
                mrun = jnp.maximum(mrun, s[:, j * LANES:(j + 1) * LANES])
        s_new = lax.dot_general(qg[g], kn[:, g * HEAD_DIM:(g + 1) * HEAD_DIM], _NT,
                                preferred_element_type=F32) + tile_bias(new_lo, LANES)
        s_ref[rows, new_lo:] = s_new
        ms.append(jnp.max(jnp.maximum(mrun, s_new), axis=-1, keepdims=True))

    _wait_pages(cv_hbm, vbuf, slot, vsem)
    for g in range(N_KV_HEADS):
        rows = slice(g * half, (g + 1) * half)
        lrun = jnp.zeros((half, LANES), F32)
        acc = jnp.zeros((half, HEAD_DIM), F32)
        for c in range(nchunk):
            p = jnp.exp(s_ref[rows, c * cw:(c + 1) * cw] - ms[g])
            for j in range(cw // LANES):
                lrun = lrun + p[:, j * LANES:(j + 1) * LANES]
            acc = acc + lax.dot_general(p.astype(BF16), chunk_t(vbuf, c, g), _NT,
                                        preferred_element_type=F32)
        p = jnp.exp(s_ref[rows, new_lo:] - ms[g])
        lrun = lrun + p
        acc = acc + jnp.dot(p.astype(BF16), vn[:, g * HEAD_DIM:(g + 1) * HEAD_DIM],
                            preferred_element_type=F32)
        o = acc / jnp.sum(lrun, axis=-1, keepdims=True)
        for hh in range(GROUP):
            h = GROUP * g + hh
            a_ref[:, h * HEAD_DIM:(h + 1) * HEAD_DIM] = o[hh * td:(hh + 1) * td, :].astype(BF16)


def _sample_attention(page_table, q, bias, k_new, v_new, cache_kt, cache_vt, td, chunk_pages=4):
    bd, npages = page_table.shape
    page = cache_kt.shape[3]
    nk = bias.shape[1]
    rowb = lambda b, pt: (b, 0)
    grid_spec = pltpu.PrefetchScalarGridSpec(
        num_scalar_prefetch=1, grid=(bd,),
        in_specs=[
            pl.BlockSpec((td, ATTN_WIDTH), rowb),
            pl.BlockSpec((td, nk), rowb),
            pl.BlockSpec((td, KV_WIDTH), rowb),
            pl.BlockSpec((td, KV_WIDTH), rowb),
            pl.BlockSpec(memory_space=pl.ANY),
            pl.BlockSpec(memory_space=pl.ANY),
        ],
        out_specs=pl.BlockSpec((td, ATTN_WIDTH), rowb),
        scratch_shapes=[
            pltpu.VMEM((2, npages, N_KV_HEADS, HEAD_DIM, page), F32),
            pltpu.VMEM((2, npages, N_KV_HEADS, HEAD_DIM, page), F32),
            pltpu.VMEM((N_HEADS * td, nk), F32),
            pltpu.SemaphoreType.DMA((2,)),
            pltpu.SemaphoreType.DMA((2,)),
        ],
    )
    return pl.pallas_call(
        functools.partial(_sample_attn_kernel, chunk_pages=chunk_pages),
        grid_spec=grid_spec, out_shape=jax.ShapeDtypeStruct((bd * td, ATTN_WIDTH), BF16),
        compiler_params=_cparams(("arbitrary",)), name="sample_dsa",
    )(page_table, q, bias, k_new, v_new, cache_kt, cache_vt)


def _layer_weights(i, g_mix, w_in, g_idx_k, b_idx_k, w_dw, b_dw, g_conv_ln, b_conv_ln, w_out, g_ffn, w_up,
                   w_ffn_conv, b_ffn_conv, w_down, g_ple, w_ple_gate, w_ple, g_final):
    d = w_in.shape[1]
    w = w_in[i]
    n_qkvi = C_KI + IDX_DIM + IDX_HEADS
    pad = jnp.zeros((d, LANES - IDX_DIM - IDX_HEADS), w.dtype)
    w_comb = jnp.concatenate([w[:, :n_qkvi], pad, w[:, n_qkvi:]], axis=1).astype(BF16)
    zpad = jnp.zeros((LANES - IDX_DIM,), F32)
    gik = jnp.concatenate([g_idx_k[i], zpad])[None]
    bik = jnp.concatenate([b_idx_k[i], zpad])[None]
    proj = (g_mix[i][None], w_comb, gik, bik)

    wo = w_out[i].astype(BF16)
    conv = (w_dw[i], b_dw[i][None], g_conv_ln[i][None], b_conv_ln[i][None], wo[:ATTN_WIDTH], wo[ATTN_WIDTH:])

    d_ff = w_down.shape[1]
    nch = d_ff // FFN_CW
    split_cols = lambda m: m.reshape(m.shape[0], nch, FFN_CW).swapaxes(0, 1)
    wu = w_up[i].astype(BF16)
    wf = w_ffn_conv[i]
    bf = b_ffn_conv[i][None]
    ffn = (g_ffn[i][None], split_cols(wu[:, :d_ff]), split_cols(wu[:, d_ff:]),
           split_cols(wf[:, :d_ff]), split_cols(wf[:, d_ff:]), split_cols(bf[:, :d_ff]), split_cols(bf[:, d_ff:]),
           w_down[i].astype(BF16).reshape(nch, FFN_CW, d), g_ple[i][None], w_ple_gate[i].astype(BF16),
           w_ple[i].astype(BF16), g_final[None])
    return proj, conv, ffn


def _ffn_state_in(state, nch):
    rows = state.shape[0]
    return state.reshape(rows, 2 * nch, FFN_CW).swapaxes(0, 1)


def _ffn_state_out(st):
    n2, rows, cw = st.shape
    return st.swapaxes(0, 1).reshape(rows, n2 * cw)


def kernel(x_prompt, x_sample, p_prompt, p_sample, cache_k, cache_v, cache_idx_k, state_conv, state_ffn_conv,
           page_table, g_mix, w_in, g_idx_k, b_idx_k, w_dw, b_dw, g_conv_ln, b_conv_ln, w_out, g_ffn, w_up,
           w_ffn_conv, b_ffn_conv, w_down, g_ple, w_ple_gate, w_ple, g_final):
    B, S, D = x_prompt.shape
    Bd, Td, _ = x_sample.shape
    depth = w_in.shape[0]
    n_pages = page_table.shape[1]
    page = cache_k.shape[2]
    past_len = n_pages * page
    d_ff = w_down.shape[1]
    nch = d_ff // FFN_CW
    topk_prompt = min(TOPK_MAX, S // 4)
    topk_sample = min(TOPK_MAX, (past_len + Td) // 4)
    assert S % KC == 0 and Td == SUBLANES and d_ff % FFN_CW == 0 and page == LANES

    tm_p = KC
    BL = min(32, Bd)
    G = Bd // BL
    tm_s = Td * BL

    tabs_p = _rope_tables(jnp.arange(S, dtype=I32))
    pos_s = past_len + jnp.arange(Td, dtype=I32)
    tabs_s = tuple(jnp.tile(t, (Bd, 1)) for t in _rope_tables(pos_s))

    def to_tm(a):
        return a.reshape(G, BL, Td, a.shape[-1]).transpose(0, 2, 1, 3).reshape(G * tm_s, a.shape[-1])

    def from_tm(a):
        return a.reshape(G, Td, BL, a.shape[-1]).transpose(0, 2, 1, 3).reshape(Bd, Td, a.shape[-1])

    hp = x_prompt.reshape(B * S, D)
    hs = x_sample.reshape(Bd * Td, D)
    outs = {k: [] for k in ("kp", "vp", "ip", "cp", "fp", "ks", "vs", "is", "cs", "fs")}
    for i in range(depth):
        last = i == depth - 1
        proj_w, conv_w, ffn_w = _layer_weights(
            i, g_mix, w_in, g_idx_k, b_idx_k, w_dw, b_dw, g_conv_ln, b_conv_ln, w_out, g_ffn, w_up,
            w_ffn_conv, b_ffn_conv, w_down, g_ple, w_ple_gate, w_ple, g_final)

        q, qit, kt, vt, kit, ktb, vb, kib, wit, glu = _project(
            hp, tabs_p, S // tm_p, *proj_w, tm=tm_p, prompt=True, batch=B)
        a = _prompt_attention(q, qit, wit, kib, ktb, vb, B, S, topk_prompt)
        nt = S // tm_p
        halo_rows = 32
        halo_spec = pl.BlockSpec(
            (halo_rows, CONV_CH), lambda b, t: (jnp.maximum((b * nt + t) * (tm_p // halo_rows) - 1, 0), 0))
        h1 = _conv_out(glu, glu, halo_spec, a, hp, *conv_w, groups=B, nt=nt, tm=tm_p, shift=1, zero_first=True)
        st0 = jnp.zeros((B, 2 * nch, SUBLANES, FFN_CW), F32)
        hp, st = _ffn(h1, p_prompt[i].reshape(B * S, -1), st0, ffn_w, groups=B, nt=nt, tm=tm_p,
                      shift=1, halo=SUBLANES, final=last)
        outs["kp"].append(kt.transpose(0, 3, 1, 2))
        outs["vp"].append(vt.transpose(0, 3, 1, 2))
        outs["ip"].append(kit.transpose(0, 2, 1))
        outs["cp"].append(glu.reshape(B, S, CONV_CH)[:, S - (CONV_W - 1):])
        outs["fp"].append(jax.vmap(_ffn_state_out)(st)[:, SUBLANES - (FFN_CONV_W - 1):])

        tm_sp = min(512, Bd * Td)
        q, qi, k, v, ki, kib, wi, glu = _project(hs, tabs_s, Bd * Td // tm_sp, *proj_w, tm=tm_sp, prompt=False)
        keys = _sample_scores(page_table, qi, wi, kib, cache_idx_k[i].transpose(0, 2, 1), Td)
        bias = _select(keys, topk_sample, rows=min(64, Bd * Td))
        a = _sample_attention(page_table, q, bias, k, v,
                              cache_k[i].transpose(0, 2, 3, 1), cache_v[i].transpose(0, 2, 3, 1), Td)
        sc = state_conv[i]
        halo_s = sc.reshape(G, BL, CONV_W - 1, CONV_CH).transpose(0, 2, 1, 3).reshape(G * (CONV_W - 1) * BL, CONV_CH)
        halo_spec = pl.BlockSpec(((CONV_W - 1) * BL, CONV_CH), lambda g, t: (g, 0))
        h1 = _conv_out(to_tm(glu), halo_s, halo_spec, to_tm(a), to_tm(hs), *conv_w, groups=G, nt=1, tm=tm_s,
                       shift=BL, zero_first=False)
        sf = state_ffn_conv[i]
        sf = sf.reshape(G, BL, FFN_CONV_W - 1, 2 * d_ff).transpose(0, 2, 1, 3).reshape(G, (FFN_CONV_W - 1) * BL, 2 * d_ff)
        st0 = jax.vmap(lambda s: _ffn_state_in(s, nch))(sf)
        hs_tm, st = _ffn(h1, to_tm(p_sample[i].reshape(Bd * Td, -1)), st0, ffn_w, groups=G, nt=1, tm=tm_s,
                         shift=BL, halo=(FFN_CONV_W - 1) * BL, final=last)
        hs = from_tm(hs_tm).reshape(Bd * Td, D)
        fs = jax.vmap(_ffn_state_out)(st)
        fs = fs.reshape(G, FFN_CONV_W - 1, BL, 2 * d_ff).transpose(0, 2, 1, 3).reshape(Bd, FFN_CONV_W - 1, 2 * d_ff)
        outs["ks"].append(k.reshape(Bd, Td, N_KV_HEADS, HEAD_DIM))
        outs["vs"].append(v.reshape(Bd, Td, N_KV_HEADS, HEAD_DIM))
        outs["is"].append(ki.reshape(Bd, Td, IDX_DIM))
        outs["cs"].append(jnp.concatenate([sc, glu.reshape(Bd, Td, CONV_CH)], axis=1)[:, Td:])
        outs["fs"].append(fs)

    y_prompt = hp.reshape(B, S, D)
    y_sample = hs.reshape(Bd, Td, D)
    st = lambda name: jnp.stack(outs[name])
    return (y_prompt, y_sample, st("kp"), st("vp"), st("ip"), st("cp"), st("fp"),
            st("ks"), st("vs"), st("is"), st("cs"), st("fs"))
```

```python
import functools

import jax
import jax.numpy as jnp
from jax import lax
from jax.experimental import pallas as pl
from jax.experimental.pallas import tpu as pltpu

N_HEADS = 8
HEAD_DIM = 64
N_KV_HEADS = 2
GROUP = N_HEADS // N_KV_HEADS
ATTN_WIDTH = N_HEADS * HEAD_DIM
KV_WIDTH = N_KV_HEADS * HEAD_DIM
IDX_HEADS = 8
IDX_DIM = 64
IDX_ROPE_DIM = 32
TOPK_MAX = 256
CONV_W = 31
FFN_CONV_W = 3
ROPE_THETA = 10000.0
EPS = 1e-6
NEG = -1e30

LANES = 128
SUBLANES = 8
INT_MIN = -2 ** 31
VMEM_LIMIT = 56 * 1024 * 1024

TQ = LANES
TPC = 4
KC = TPC * LANES
BISECT_FLOAT_PASSES = 16
BISECT_CHECK_EVERY = 4
BISECT_MAX_PASSES = 48

F32 = jnp.float32
BF16 = jnp.bfloat16
I32 = jnp.int32

_NT = (((1,), (1,)), ((), ()))


def _cparams(sem):
    return pltpu.CompilerParams(dimension_semantics=sem, vmem_limit_bytes=VMEM_LIMIT)


def _sortable(x):
    b = pltpu.bitcast(x, I32)
    return b ^ ((b >> 31) & jnp.int32(0x7FFFFFFF))


def _rmsnorm(x, g):
    return x * lax.rsqrt(jnp.mean(x * x, axis=-1, keepdims=True) + EPS) * g


C_Q = 0
C_K = C_Q + ATTN_WIDTH
C_V = C_K + KV_WIDTH
C_QI = C_V + KV_WIDTH
C_KI = C_QI + IDX_HEADS * IDX_DIM
C_A = C_KI + LANES
CONV_CH = 512
C_G = C_A + CONV_CH
C_END = C_G + CONV_CH


def _proj_kernel(x_ref, g_ref, w_ref, gik_ref, bik_ref, cq_ref, sq_ref, ci_ref, si_ref, *out_refs,
                 wi_scale, prompt):
    tm = x_ref.shape[0]
    hn = _rmsnorm(x_ref[...], g_ref[...]).astype(BF16)
    z = jnp.dot(hn, w_ref[...], preferred_element_type=F32)

    lane = lax.broadcasted_iota(I32, (tm, LANES), 1)
    in_head = lane % HEAD_DIM
    cq, sq, ci, si = cq_ref[...], sq_ref[...], ci_ref[...], si_ref[...]

    def rope_full(xg):
        sw = jnp.where(in_head < HEAD_DIM // 2, pltpu.roll(xg, LANES - HEAD_DIM // 2, 1),
                       pltpu.roll(xg, HEAD_DIM // 2, 1))
        return xg * cq + sw * sq

    def rope_part(xg):
        sw = jnp.where(in_head < IDX_ROPE_DIM // 2, pltpu.roll(xg, LANES - IDX_ROPE_DIM // 2, 1),
                       pltpu.roll(xg, IDX_ROPE_DIM // 2, 1))
        return xg * ci + sw * si

    q_groups = [rope_full(z[:, C_Q + g * LANES:C_Q + (g + 1) * LANES]) * (HEAD_DIM ** -0.5)
                for g in range(ATTN_WIDTH // LANES)]
    qi_groups = [rope_part(z[:, C_QI + g * LANES:C_QI + (g + 1) * LANES])
                 for g in range(IDX_HEADS * IDX_DIM // LANES)]
    kr = rope_full(z[:, C_K:C_K + LANES])
    vr = z[:, C_V:C_V + LANES]

    zg = z[:, C_KI:C_KI + LANES]
    is_ki = lane < IDX_DIM
    mu = jnp.sum(jnp.where(is_ki, zg, 0.0), axis=-1, keepdims=True) / IDX_DIM
    xc = jnp.where(is_ki, zg - mu, 0.0)
    var = jnp.sum(xc * xc, axis=-1, keepdims=True) / IDX_DIM
    kin = rope_part(xc * lax.rsqrt(var + EPS) * gik_ref[...] + bik_ref[...])
    wig = zg * wi_scale
    glu = z[:, C_A:C_A + CONV_CH] * jax.nn.sigmoid(z[:, C_G:C_G + CONV_CH])

    if prompt:
        q_ref, qit_ref, kt_ref, vt_ref, kit_ref, ktb_ref, vb_ref, kib_ref, wit_ref, glu_ref = out_refs
        for g, qg in enumerate(q_groups):
            q_ref[:, g * LANES:(g + 1) * LANES] = qg.astype(BF16)
        for g, qig in enumerate(qi_groups):
            qit_ref[g * LANES:(g + 1) * LANES, :] = qig.T.astype(BF16)
        krt = kr.T
        vrt = vr.T
        for g in range(N_KV_HEADS):
            rows = slice(g * HEAD_DIM, (g + 1) * HEAD_DIM)
            kt_ref[0, g] = krt[rows]
            vt_ref[0, g] = vrt[rows]
            ktb_ref[0, g] = krt[rows].astype(BF16)
            vb_ref[g] = vr[:, rows].astype(BF16)
        kit_ref[0] = kin.T[:IDX_DIM]
        kib_ref[...] = kin[:, :IDX_DIM].astype(BF16)
        wit_ref[...] = wig.T[IDX_DIM:IDX_DIM + IDX_HEADS, :]
        glu_ref[...] = glu
    else:
        q_ref, qi_ref, k_ref, v_ref, ki_ref, kib_ref, wi_ref, glu_ref = out_refs
        for g, qg in enumerate(q_groups):
            q_ref[:, g * LANES:(g + 1) * LANES] = qg.astype(BF16)
        for g, qig in enumerate(qi_groups):
            qi_ref[:, g * LANES:(g + 1) * LANES] = qig.astype(BF16)
        k_ref[...] = kr
        v_ref[...] = vr
        ki_ref[...] = kin[:, :IDX_DIM]
        kib_ref[...] = kin[:, :IDX_DIM].astype(BF16)
        wi_ref[...] = wig[:, IDX_DIM:IDX_DIM + IDX_HEADS]
        glu_ref[...] = glu


def _rope_tables(pos):
    def cs(half):
        inv = jnp.power(jnp.float32(ROPE_THETA), -jnp.arange(half, dtype=F32) / half)
        ang = pos.astype(F32)[:, None] * inv[None, :]
        return jnp.cos(ang), jnp.sin(ang)

    c, s = cs(HEAD_DIM // 2)
    cq = jnp.concatenate([c, c, c, c], axis=-1)
    sq = jnp.concatenate([-s, s, -s, s], axis=-1)
    c2, s2 = cs(IDX_ROPE_DIM // 2)
    one = jnp.ones((pos.shape[0], IDX_DIM - IDX_ROPE_DIM), F32)
    ci = jnp.concatenate([c2, c2, one, c2, c2, one], axis=-1)
    si = jnp.concatenate([-s2, s2, 0 * one, -s2, s2, 0 * one], axis=-1)
    return cq, sq, ci, si


def _project(x, tables, tab_blocks, g_mix, w_comb, gik, bik, *, tm, prompt, batch=None):
    n, d = x.shape
    nb = n // tm
    row = lambda i: (i, 0)
    const = lambda i: (0, 0)
    tab = lambda i: (i % tab_blocks, 0)
    wi_scale = IDX_HEADS ** -0.5 * IDX_DIM ** -0.5
    sds = jax.ShapeDtypeStruct
    if prompt:
        seq = n // batch
        nt = seq // tm
        bt = lambda i: (i // nt, 0, 0, i % nt)
        out_shape = (
            sds((n, ATTN_WIDTH), BF16),
            sds((IDX_HEADS * IDX_DIM, n), BF16),
            sds((batch, N_KV_HEADS, HEAD_DIM, seq), F32),
            sds((batch, N_KV_HEADS, HEAD_DIM, seq), F32),
            sds((batch, IDX_DIM, seq), F32),
            sds((nb, N_KV_HEADS, HEAD_DIM, tm), BF16),
            sds((N_KV_HEADS, n, HEAD_DIM), BF16),
            sds((n, IDX_DIM), BF16),
            sds((IDX_HEADS, n), F32),
            sds((n, CONV_CH), F32),
        )
        out_specs = (
            pl.BlockSpec((tm, ATTN_WIDTH), row),
            pl.BlockSpec((IDX_HEADS * IDX_DIM, tm), lambda i: (0, i)),
            pl.BlockSpec((1, N_KV_HEADS, HEAD_DIM, tm), bt),
            pl.BlockSpec((1, N_KV_HEADS, HEAD_DIM, tm), bt),
            pl.BlockSpec((1, IDX_DIM, tm), lambda i: (i // nt, 0, i % nt)),
            pl.BlockSpec((1, N_KV_HEADS, HEAD_DIM, tm), lambda i: (i, 0, 0, 0)),
            pl.BlockSpec((N_KV_HEADS, tm, HEAD_DIM), lambda i: (0, i, 0)),
            pl.BlockSpec((tm, IDX_DIM), row),
            pl.BlockSpec((IDX_HEADS, tm), lambda i: (0, i)),
            pl.BlockSpec((tm, CONV_CH), row),
        )
    else:
        out_shape = (
            sds((n, ATTN_WIDTH), BF16),
            sds((n, IDX_HEADS * IDX_DIM), BF16),
            sds((n, KV_WIDTH), F32),
            sds((n, KV_WIDTH), F32),
            sds((n, IDX_DIM), F32),
            sds((n, IDX_DIM), BF16),
            sds((n, IDX_HEADS), F32),
            sds((n, CONV_CH), F32),
        )
        out_specs = (
            pl.BlockSpec((tm, ATTN_WIDTH), row),
            pl.BlockSpec((tm, IDX_HEADS * IDX_DIM), row),
            pl.BlockSpec((tm, KV_WIDTH), row),
            pl.BlockSpec((tm, KV_WIDTH), row),
            pl.BlockSpec((tm, IDX_DIM), row),
            pl.BlockSpec((tm, IDX_DIM), row),
            pl.BlockSpec((tm, IDX_HEADS), row),
            pl.BlockSpec((tm, CONV_CH), row),
        )
    in_specs = [
        pl.BlockSpec((tm, d), row),
        pl.BlockSpec((1, d), const),
        pl.BlockSpec((d, C_END), const),
        pl.BlockSpec((1, LANES), const),
        pl.BlockSpec((1, LANES), const),
    ] + [pl.BlockSpec((tm, LANES), tab)] * 4
    return pl.pallas_call(
        functools.partial(_proj_kernel, wi_scale=wi_scale, prompt=prompt),
        grid=(nb,), in_specs=in_specs, out_specs=out_specs, out_shape=out_shape,
        compiler_params=_cparams(("parallel",)), name="in_proj",
    )(x, g_mix, w_comb, gik, bik, *tables)


def _tile_count(pred):
    return jnp.sum(pred.astype(I32).reshape(LANES // SUBLANES, SUBLANES, LANES), axis=0)


def _count_keys(key_ref, nch, pred):
    def body(c, accs):
        return tuple(a + _tile_count(pred(key_ref[c * TPC + j], c * TPC + j)) for j, a in enumerate(accs))
    z8 = jnp.zeros((SUBLANES, LANES), I32)
    accs = lax.fori_loop(0, nch, body, (z8,) * TPC)
    return jnp.sum(sum(accs[1:], accs[0]), axis=0, keepdims=True)


def _prompt_attn_kernel(qit_ref, wit_ref, q_ref, kib_ref, ktb_ref, vb_ref, a_ref,
                        key_ref, bias_ref, s_ref, m_ref, l_ref, acc_ref, *, topk):
    i = pl.program_id(1)
    nch = i // TPC + 1
    r_io = lax.broadcasted_iota(I32, (TQ, TQ), 0)
    c_io = lax.broadcasted_iota(I32, (TQ, TQ), 1)

    def causal(kt):
        return r_io + (kt - i) * TQ <= c_io

    qit = jnp.concatenate([qit_ref[h * IDX_DIM:(h + 1) * IDX_DIM, :] for h in range(IDX_HEADS)], axis=1)
    w_rows = [wit_ref[h:h + 1, :] for h in range(IDX_HEADS)]

    def score_body(c, carry):
        for j in range(TPC):
            kt = c * TPC + j
            kit = kib_ref[pl.ds(pl.multiple_of(kt * TQ, TQ), TQ), :]
            lg = jnp.dot(kit, qit, preferred_element_type=F32)
            sc = jnp.maximum(lg[:, :TQ], 0.0) * w_rows[0]
            for h in range(1, IDX_HEADS):
                sc = sc + jnp.maximum(lg[:, h * TQ:(h + 1) * TQ], 0.0) * w_rows[h]
            key_ref[kt] = jnp.where(causal(kt), _sortable(sc), INT_MIN)
        return carry
    lax.fori_loop(0, nch, score_body, 0)

    def minmax_body(c, mm):
        kmin, kmax = mm
        for j in range(TPC):
            k = key_ref[c * TPC + j]
            kmax = jnp.maximum(kmax, k)
            kmin = jnp.minimum(kmin, jnp.where(k == INT_MIN, jnp.int32(2 ** 31 - 1), k))
        return kmin, kmax
    kmin, kmax = lax.fori_loop(0, nch, minmax_body, (jnp.full((TQ, TQ), 2 ** 31 - 1, I32),
                                                     jnp.full((TQ, TQ), INT_MIN, I32)))
    n_keys = i * TQ + c_io[0:1, :] + 1
    enough = n_keys >= topk
    lo0 = jnp.min(kmin, axis=0, keepdims=True)
    hi0 = jnp.max(kmax, axis=0, keepdims=True) + 1

    def unsort(k):
        return pltpu.bitcast(k ^ ((k >> 31) & jnp.int32(0x7FFFFFFF)), F32)

    def bisect_pass(p, lo, hi, cnt):
        mid_i = (lo >> 1) + (hi >> 1) + (lo & hi & 1)
        mid_f = _sortable(0.5 * unsort(lo) + 0.5 * unsort(hi))
        use_f = (p < BISECT_FLOAT_PASSES) & (mid_f > lo) & (mid_f < hi)
        mid = jnp.where(use_f, mid_f, mid_i)
        c = _count_keys(key_ref, nch, lambda k, kt: k >= mid)
        ge = c >= topk
        return jnp.where(ge, mid, lo), jnp.where(ge, hi, mid), jnp.where(ge, c, cnt)

    def bisect_cond(st):
        p, done = st[0], st[1]
        return jnp.logical_and(jnp.logical_not(done), p < BISECT_MAX_PASSES)

    def bisect_body(st):
        p, _, lo, hi, cnt = st
        for u in range(BISECT_CHECK_EVERY):
            lo, hi, cnt = bisect_pass(p + u, lo, hi, cnt)
        settled = (cnt == topk) | (lo + 1 >= hi) | jnp.logical_not(enough)
        return p + BISECT_CHECK_EVERY, jnp.min(settled.astype(I32)) > 0, lo, hi, cnt

    _, _, lo, _, _ = lax.while_loop(bisect_cond, bisect_body,
                                    (jnp.int32(0), jnp.bool_(False), lo0, hi0, n_keys))
    thr = jnp.where(enough, lo, INT_MIN)

    n_gt = _count_keys(key_ref, nch, lambda k, kt: k > thr)
    n_eq = _count_keys(key_ref, nch, lambda k, kt: k == thr)
    need = topk - n_gt
    n_idx_bits = max(1, (key_ref.shape[0] * TQ).bit_length())

    def tie_search(_):
        def body(p, m):
            cand = m | jnp.left_shift(jnp.int32(1), n_idx_bits - 1 - p)
            below = _count_keys(key_ref, nch, lambda k, kt: (k == thr) & (r_io + kt * TQ < cand))
            return jnp.where(below < need, cand, m)
        return lax.fori_loop(0, n_idx_bits, body, jnp.zeros((1, TQ), I32))

    excess = jnp.max(jnp.where((need > 0) & (thr > INT_MIN), n_eq - need, 0)) > 0
    idx_max = lax.cond(excess, tie_search,
                       lambda _: jnp.full((1, TQ), key_ref.shape[0] * TQ, I32), 0)

    def bias_body(c, carry):
        for j in range(TPC):
            kt = c * TPC + j
            k = key_ref[kt]
            sel = ((k > thr) | ((k == thr) & (r_io + kt * TQ <= idx_max))) & causal(kt)
            bias_ref[kt] = jnp.where(sel, 0.0, NEG).astype(F32).T
        return carry
    lax.fori_loop(0, nch, bias_body, 0)

    q = q_ref[...]
    q4 = [jnp.concatenate([q[:, (GROUP * g + hh) * HEAD_DIM:(GROUP * g + hh + 1) * HEAD_DIM]
                           for hh in range(GROUP)], axis=0) for g in range(N_KV_HEADS)]
    m_ref[...] = jnp.full(m_ref.shape, NEG, F32)

    def qk_body(c, carry):
        bias = jnp.concatenate([bias_ref[c * TPC + j] for j in range(TPC)], axis=1)
        for g in range(N_KV_HEADS):
            s = jnp.dot(q4[g], ktb_ref[c, g], preferred_element_type=F32)
            s = (s.reshape(GROUP, TQ, KC) + bias[None]).reshape(GROUP * TQ, KC)
            s_ref[g, c] = s
            cm = s[:, :TQ]
            for j in range(1, TPC):
                cm = jnp.maximum(cm, s[:, j * TQ:(j + 1) * TQ])
            m_ref[g] = jnp.maximum(m_ref[g], cm)
        return carry
    lax.fori_loop(0, nch, qk_body, 0)
    m = [jnp.max(m_ref[g], axis=-1, keepdims=True) for g in range(N_KV_HEADS)]
    l_ref[...] = jnp.zeros(l_ref.shape, F32)
    acc_ref[...] = jnp.zeros(acc_ref.shape, F32)

    def pv_body(c, carry):
        for g in range(N_KV_HEADS):
            p = jnp.exp(s_ref[g, c] - m[g])
            ps = p[:, :TQ]
            for j in range(1, TPC):
                ps = ps + p[:, j * TQ:(j + 1) * TQ]
            l_ref[g] += ps
            vc = vb_ref[g, pl.ds(pl.multiple_of(c * KC, KC), KC), :]
            acc_ref[g] += jnp.dot(p.astype(BF16), vc, preferred_element_type=F32)
        return carry
    lax.fori_loop(0, nch, pv_body, 0)
    for g in range(N_KV_HEADS):
        o = acc_ref[g] / jnp.sum(l_ref[g], axis=-1, keepdims=True)
        for hh in range(GROUP):
            h = GROUP * g + hh
            a_ref[:, h * HEAD_DIM:(h + 1) * HEAD_DIM] = o[hh * TQ:(hh + 1) * TQ, :].astype(BF16)


def _prompt_attention(q, qit, wit, kib, ktb, vb, batch, seq, topk):
    nb = seq // TQ
    ncs = seq // KC
    qrow = lambda b, i: (b * nb + i, 0)
    qcol = lambda b, i: (0, b * nb + i)
    return pl.pallas_call(
        functools.partial(_prompt_attn_kernel, topk=topk),
        grid=(batch, nb),
        in_specs=[
            pl.BlockSpec((IDX_HEADS * IDX_DIM, TQ), qcol),
            pl.BlockSpec((IDX_HEADS, TQ), qcol),
            pl.BlockSpec((TQ, ATTN_WIDTH), qrow),
            pl.BlockSpec((seq, IDX_DIM), lambda b, i: (b, 0)),
            pl.BlockSpec((ncs, N_KV_HEADS, HEAD_DIM, KC), lambda b, i: (b, 0, 0, 0)),
            pl.BlockSpec((N_KV_HEADS, seq, HEAD_DIM), lambda b, i: (0, b, 0)),
        ],
        out_specs=pl.BlockSpec((TQ, ATTN_WIDTH), qrow),
        out_shape=jax.ShapeDtypeStruct((batch * seq, ATTN_WIDTH), BF16),
        scratch_shapes=[
            pltpu.VMEM((nb, TQ, TQ), I32),
            pltpu.VMEM((nb, TQ, TQ), F32),
            pltpu.VMEM((N_KV_HEADS, ncs, GROUP * TQ, KC), F32),
            pltpu.VMEM((N_KV_HEADS, GROUP * TQ, TQ), F32),
            pltpu.VMEM((N_KV_HEADS, GROUP * TQ, TQ), F32),
            pltpu.VMEM((N_KV_HEADS, GROUP * TQ, HEAD_DIM), F32),
        ],
        compiler_params=_cparams(("parallel", "arbitrary")), name="prompt_dsa",
    )(qit, wit, q, kib, ktb, vb)


def _conv_out_kernel(glu_ref, halo_ref, a_ref, x_ref, wdw_ref, bdw_ref, gln_ref, bln_ref, wa_ref, wc_ref,
                     h_ref, xp_ref, c_ref, *maybe_xs_ref, shift, zero_first, rc):
    tm = glu_ref.shape[0]
    halo = halo_ref.shape[0]
    hv = halo_ref[...]
    if zero_first:
        hv = jnp.where(pl.program_id(1) == 0, 0.0, hv)
    xp_ref[0:halo, :] = hv
    xp_ref[halo:halo + tm, :] = glu_ref[...]
    if shift % SUBLANES:
        (xs_ref,) = maybe_xs_ref
        for r in range(1, SUBLANES):
            xs_ref[r - 1] = xp_ref[r:r + xs_ref.shape[1], :]

    def window(off, ls):
        r = off % SUBLANES
        if r == 0:
            return xp_ref[off:off + rc, ls]
        return xs_ref[r - 1, off - r:off - r + rc, ls]

    for lg in range(CONV_CH // LANES):
        ls = slice(lg * LANES, (lg + 1) * LANES)
        for r0 in range(0, tm, rc):
            acc = jnp.zeros((rc, LANES), F32)
            for j in range(CONV_W):
                off = halo - (CONV_W - 1 - j) * shift
                acc = acc + wdw_ref[j:j + 1, ls] * window(r0 + off, ls)
            c_ref[r0:r0 + rc, ls] = acc + bdw_ref[:, ls]

    c = c_ref[...]
    mu = jnp.mean(c, axis=-1, keepdims=True)
    xc = c - mu
    var = jnp.mean(xc * xc, axis=-1, keepdims=True)
    y = xc * lax.rsqrt(var + EPS) * gln_ref[...] + bln_ref[...]
    y = jax.nn.silu(y)
    h_ref[...] = (x_ref[...]
                  + jnp.dot(a_ref[...], wa_ref[...], preferred_element_type=F32)
                  + jnp.dot(y.astype(BF16), wc_ref[...], preferred_element_type=F32))


def _conv_out(glu, halo_arr, halo_spec, a, x, w_dw, b_dw, g_ln, b_ln, w_a, w_c, *, groups, nt, tm,
              shift, zero_first):
    d = x.shape[1]
    halo = halo_spec.block_shape[0]
    row = lambda b, i: (b * nt + i, 0)
    const = lambda b, i: (0, 0)
    return pl.pallas_call(
        functools.partial(_conv_out_kernel, shift=shift, zero_first=zero_first, rc=32),
        grid=(groups, nt),
        in_specs=[
            pl.BlockSpec((tm, CONV_CH), row),
            halo_spec,
            pl.BlockSpec((tm, ATTN_WIDTH), row),
            pl.BlockSpec((tm, d), row),
            pl.BlockSpec((CONV_W, CONV_CH), const),
            pl.BlockSpec((1, CONV_CH), const),
            pl.BlockSpec((1, CONV_CH), const),
            pl.BlockSpec((1, CONV_CH), const),
            pl.BlockSpec((ATTN_WIDTH, d), const),
            pl.BlockSpec((CONV_CH, d), const),
        ],
        out_specs=pl.BlockSpec((tm, d), row),
        out_shape=jax.ShapeDtypeStruct(x.shape, F32),
        scratch_shapes=[pltpu.VMEM((halo + tm, CONV_CH), F32), pltpu.VMEM((tm, CONV_CH), F32)]
        + ([pltpu.VMEM((SUBLANES - 1, halo + tm - SUBLANES, CONV_CH), F32)] if shift % SUBLANES else []),
        compiler_params=_cparams(("parallel", "arbitrary")), name="conv_out_proj",
    )(glu, halo_arr, a, x, w_dw, b_dw, g_ln, b_ln, w_a, w_c)


FFN_CW = 256
FFN_RC = 32


def _ffn_kernel(h_ref, p_ref, st_ref, gffn_ref, wu_ref, wf_ref, bf_ref, wdn_ref, gple_ref, wgate_ref, wple_ref,
                gfin_ref, y_ref, sto_ref, u_ref, act_ref, carry_ref, *, shift, halo, final):
    tm = h_ref.shape[0]
    nch = wu_ref.shape[0]
    cw = wu_ref.shape[2] // 2
    first = pl.program_id(1) == 0
    h = h_ref[...]
    hn = _rmsnorm(h, gffn_ref[...]).astype(BF16)

    @pl.when(first)
    def _():
        carry_ref[...] = st_ref[...]

    def up(c, buf):
        ub = u_ref.at[buf]
        ub[0:halo, :] = carry_ref[c]
        ub[halo:halo + tm, :] = jnp.dot(hn, wu_ref[c], preferred_element_type=F32)
        tail = ub[tm:tm + halo, :]
        carry_ref[c] = tail
        sto_ref[c] = tail

    def act(c, buf):
        wf = wf_ref[c]
        bf = bf_ref[c]
        for r0 in range(0, tm, FFN_RC):
            out = bf
            for k in range(FFN_CONV_W):
                off = r0 + halo - (FFN_CONV_W - 1 - k) * shift
                out = out + wf[k:k + 1, :] * u_ref[buf, off:off + FFN_RC, :]
            act_ref[c, r0:r0 + FFN_RC, :] = (jax.nn.silu(out[:, :cw]) * out[:, cw:]).astype(BF16)

    up(0, 0)
    n_pairs = (nch - 1) // 2

    def pair(k, carry):
        up(2 * k + 1, 1)
        act(2 * k, 0)
        up(2 * k + 2, 0)
        act(2 * k + 1, 1)
        return carry
    lax.fori_loop(0, n_pairs, pair, 0)
    if (nch - 1) % 2:
        up(nch - 1, 1)
        act(nch - 2, 0)
        act(nch - 1, 1)
    else:
        act(nch - 1, 0)

    down = jnp.dot(act_ref[0], wdn_ref[0], preferred_element_type=F32)
    for c in range(1, nch):
        down = down + jnp.dot(act_ref[c], wdn_ref[c], preferred_element_type=F32)
    h2 = h + down
    gate = jax.nn.sigmoid(jnp.dot(_rmsnorm(h2, gple_ref[...]).astype(BF16), wgate_ref[...],
                                  preferred_element_type=F32))
    ple = jnp.dot(p_ref[...].astype(BF16), wple_ref[...], preferred_element_type=F32)
    h3 = h2 + ple * gate
    y_ref[...] = _rmsnorm(h3, gfin_ref[...]) if final else h3


def _ffn(h, p, st, wts, *, groups, nt, tm, shift, halo, final):
    (g_ffn, wu, wf, bf, wdn, g_ple, w_gate, w_ple, g_fin) = wts
    d = h.shape[1]
    nch, _, cw2 = wu.shape
    row = lambda b, i: (b * nt + i, 0)
    c2 = lambda b, i: (0, 0)
    c3 = lambda b, i: (0, 0, 0)
    stspec = pl.BlockSpec((None, nch, halo, cw2), lambda b, i: (b, 0, 0, 0))
    once = dict(pipeline_mode=pl.Buffered(1))
    return pl.pallas_call(
        functools.partial(_ffn_kernel, shift=shift, halo=halo, final=final),
        grid=(groups, nt),
        in_specs=[
            pl.BlockSpec((tm, d), row),
            pl.BlockSpec((tm, p.shape[1]), row),
            stspec,
            pl.BlockSpec((1, d), c2),
            pl.BlockSpec(wu.shape, c3, **once),
            pl.BlockSpec(wf.shape, c3),
            pl.BlockSpec(bf.shape, c3),
            pl.BlockSpec(wdn.shape, c3, **once),
            pl.BlockSpec((1, d), c2),
            pl.BlockSpec(w_gate.shape, c2, **once),
            pl.BlockSpec(w_ple.shape, c2, **once),
            pl.BlockSpec((1, d), c2),
        ],
        out_specs=(pl.BlockSpec((tm, d), row), stspec),
        out_shape=(jax.ShapeDtypeStruct(h.shape, F32),
                   jax.ShapeDtypeStruct((groups, nch, halo, cw2), F32)),
        scratch_shapes=[
            pltpu.VMEM((2, halo + tm, cw2), F32),
            pltpu.VMEM((nch, tm, cw2 // 2), BF16),
            pltpu.VMEM((nch, halo, cw2), F32),
        ],
        compiler_params=_cparams(("parallel", "arbitrary")), name="conv_ffn_ple",
    )(h, p, st, g_ffn, wu, wf, bf, wdn, g_ple, w_gate, w_ple, g_fin)


def _start_pages(pt_ref, b, src_hbm, dst_ref, slot, sem):
    for p in range(dst_ref.shape[1]):
        pltpu.make_async_copy(src_hbm.at[pt_ref[b, p]], dst_ref.at[slot, p], sem.at[slot]).start()


def _wait_pages(src_hbm, dst_ref, slot, sem):
    for p in range(dst_ref.shape[1]):
        pltpu.make_async_copy(src_hbm.at[0], dst_ref.at[slot, p], sem.at[slot]).wait()


def _rows_by_head(x, width):
    return jnp.concatenate([x[:, h * width:(h + 1) * width] for h in range(x.shape[1] // width)], axis=0)


def _pad_rows(x, rows):
    return jnp.concatenate([x, jnp.zeros((rows - x.shape[0], x.shape[1]), x.dtype)], axis=0)


def _sample_score_kernel(pt_ref, qi_ref, wi_ref, kin_ref, cidx_hbm, key_ref, ibuf, sem, *, chunk_pages):
    b = pl.program_id(0)
    nb = pl.num_programs(0)
    slot = b % 2
    npages, page = ibuf.shape[1], ibuf.shape[3]
    td = qi_ref.shape[0]

    @pl.when(b == 0)
    def _():
        _start_pages(pt_ref, b, cidx_hbm, ibuf, 0, sem)

    @pl.when(b + 1 < nb)
    def _():
        _start_pages(pt_ref, b + 1, cidx_hbm, ibuf, 1 - slot, sem)

    qi = _rows_by_head(qi_ref[...], IDX_DIM)
    wcol = jnp.concatenate([wi_ref[:, h:h + 1] for h in range(IDX_HEADS)], axis=0)

    def head_sum(lg):
        r = jnp.maximum(lg, 0.0) * wcol
        return jnp.sum(r.reshape(IDX_HEADS, td, r.shape[1]), axis=0)

    _wait_pages(cidx_hbm, ibuf, slot, sem)
    cw = chunk_pages * page
    for c in range(npages // chunk_pages):
        kt = jnp.concatenate([ibuf[slot, c * chunk_pages + j] for j in range(chunk_pages)], axis=1)
        lg = jnp.dot(qi, kt.astype(BF16), preferred_element_type=F32)
        key_ref[:, c * cw:(c + 1) * cw] = _sortable(head_sum(lg))

    lg = lax.dot_general(qi, _pad_rows(kin_ref[...], LANES), _NT, preferred_element_type=F32)
    sc = head_sum(lg)
    t_io = lax.broadcasted_iota(I32, sc.shape, 0)
    j_io = lax.broadcasted_iota(I32, sc.shape, 1)
    key_ref[:, npages * page:] = jnp.where(j_io <= t_io, _sortable(sc), INT_MIN)


def _sample_scores(page_table, qi, wi, kib, cache_idx_t, td, chunk_pages=4):
    bd, npages = page_table.shape
    page = cache_idx_t.shape[2]
    nk = npages * page + LANES
    grid_spec = pltpu.PrefetchScalarGridSpec(
        num_scalar_prefetch=1, grid=(bd,),
        in_specs=[
            pl.BlockSpec((td, IDX_HEADS * IDX_DIM), lambda b, pt: (b, 0)),
            pl.BlockSpec((td, IDX_HEADS), lambda b, pt: (b, 0)),
            pl.BlockSpec((td, IDX_DIM), lambda b, pt: (b, 0)),
            pl.BlockSpec(memory_space=pl.ANY),
        ],
        out_specs=pl.BlockSpec((td, nk), lambda b, pt: (b, 0)),
        scratch_shapes=[pltpu.VMEM((2, npages, IDX_DIM, page), F32), pltpu.SemaphoreType.DMA((2,))],
    )
    return pl.pallas_call(
        functools.partial(_sample_score_kernel, chunk_pages=chunk_pages),
        grid_spec=grid_spec, out_shape=jax.ShapeDtypeStruct((bd * td, nk), I32),
        compiler_params=_cparams(("arbitrary",)), name="sample_scores",
    )(page_table, qi, wi, kib, cache_idx_t)


def _select_kernel(key_ref, bias_ref, *, topk):
    rows, nk = key_ref.shape
    ntile = nk // LANES

    def count(pred):
        acc = jnp.zeros((rows, LANES), I32)
        for c in range(ntile):
            acc = acc + pred(key_ref[:, c * LANES:(c + 1) * LANES], c).astype(I32)
        return jnp.sum(acc, axis=-1, keepdims=True)

    t0 = jnp.where(count(lambda k, c: k >= 0) >= topk, 0, INT_MIN).astype(I32)

    def bit_body(p, t):
        cand = t | jnp.left_shift(jnp.int32(1), 30 - p)
        return jnp.where(count(lambda k, c: k >= cand) >= topk, cand, t)
    thr = lax.fori_loop(0, 31, bit_body, t0)

    n_gt = count(lambda k, c: k > thr)
    n_eq = count(lambda k, c: k == thr)
    need = topk - n_gt
    l_io = lax.broadcasted_iota(I32, (rows, LANES), 1)
    n_idx_bits = max(1, nk.bit_length())

    def tie_search(_):
        def body(p, m):
            cand = m | jnp.left_shift(jnp.int32(1), n_idx_bits - 1 - p)
            below = count(lambda k, c: (k == thr) & (l_io + c * LANES < cand))
            return jnp.where(below < need, cand, m)
        return lax.fori_loop(0, n_idx_bits, body, jnp.zeros((rows, 1), I32))

    excess = jnp.max(jnp.where((need > 0) & (thr > INT_MIN), n_eq - need, 0)) > 0
    idx_max = lax.cond(excess, tie_search, lambda _: jnp.full((rows, 1), nk, I32), 0)

    for c in range(ntile):
        k = key_ref[:, c * LANES:(c + 1) * LANES]
        sel = ((k > thr) | ((k == thr) & (l_io + c * LANES <= idx_max))) & (k != INT_MIN)
        bias_ref[:, c * LANES:(c + 1) * LANES] = jnp.where(sel, 0.0, NEG).astype(F32)


def _select(keys, topk, rows):
    n, nk = keys.shape
    return pl.pallas_call(
        functools.partial(_select_kernel, topk=topk),
        grid=(n // rows,),
        in_specs=[pl.BlockSpec((rows, nk), lambda i: (i, 0))],
        out_specs=pl.BlockSpec((rows, nk), lambda i: (i, 0)),
        out_shape=jax.ShapeDtypeStruct((n, nk), F32),
        compiler_params=_cparams(("parallel",)), name="sample_select",
    )(keys)


def _sample_attn_kernel(pt_ref, q_ref, bias_ref, kn_ref, vn_ref, ck_hbm, cv_hbm, a_ref,
                        kbuf, vbuf, s_ref, ksem, vsem, *, chunk_pages):
    b = pl.program_id(0)
    nb = pl.num_programs(0)
    slot = b % 2
    npages, page = kbuf.shape[1], kbuf.shape[4]
    td = q_ref.shape[0]
    half = GROUP * td

    @pl.when(b == 0)
    def _():
        _start_pages(pt_ref, b, ck_hbm, kbuf, 0, ksem)
        _start_pages(pt_ref, b, cv_hbm, vbuf, 0, vsem)

    @pl.when(b + 1 < nb)
    def _():
        _start_pages(pt_ref, b + 1, ck_hbm, kbuf, 1 - slot, ksem)
        _start_pages(pt_ref, b + 1, cv_hbm, vbuf, 1 - slot, vsem)

    q = _rows_by_head(q_ref[...], HEAD_DIM)
    qg = [q[g * half:(g + 1) * half] for g in range(N_KV_HEADS)]
    cw = chunk_pages * page
    nchunk = npages // chunk_pages
    new_lo = npages * page
    kn = _pad_rows(kn_ref[...], LANES).astype(BF16)
    vn = _pad_rows(vn_ref[...], LANES).astype(BF16)

    def chunk_t(buf, c, g):
        return jnp.concatenate([buf[slot, c * chunk_pages + j, g] for j in range(chunk_pages)],
                               axis=1).astype(BF16)

    def tile_bias(lo, n):
        return jnp.concatenate([bias_ref[:, lo:lo + n]] * GROUP, axis=0)

    _wait_pages(ck_hbm, kbuf, slot, ksem)
    ms = []
    for g in range(N_KV_HEADS):
        rows = slice(g * half, (g + 1) * half)
        mrun = jnp.full((half, LANES), NEG, F32)
        for c in range(nchunk):
            s = jnp.dot(qg[g], chunk_t(kbuf, c, g), preferred_element_type=F32) + tile_bias(c * cw, cw)
            s_ref[rows, c * cw:(c + 1) * cw] = s
            for j in range(cw // LANES):
                mrun = jnp.maximum(mrun, s[:, j * LANES:(j + 1) * LANES])
        s_new = lax.dot_general(qg[g], kn[:, g * HEAD_DIM:(g + 1) * HEAD_DIM], _NT,
                                preferred_element_type=F32) + tile_bias(new_lo, LANES)
        s_ref[rows, new_lo:] = s_new
        ms.append(jnp.max(jnp.maximum(mrun, s_new), axis=-1, keepdims=True))

    _wait_pages(cv_hbm, vbuf, slot, vsem)
    for g in range(N_KV_HEADS):
        rows = slice(g * half, (g + 1) * half)
        lrun = jnp.zeros((half, LANES), F32)
        acc = jnp.zeros((half, HEAD_DIM), F32)
        for c in range(nchunk):
            p = jnp.exp(s_ref[rows, c * cw:(c + 1) * cw] - ms[g])
            for j in range(cw // LANES):
                lrun = lrun + p[:, j * LANES:(j + 1) * LANES]
            acc = acc + lax.dot_general(p.astype(BF16), chunk_t(vbuf, c, g), _NT,
                                        preferred_element_type=F32)
        p = jnp.exp(s_ref[rows, new_lo:] - ms[g])
        lrun = lrun + p
        acc = acc + jnp.dot(p.astype(BF16), vn[:, g * HEAD_DIM:(g + 1) * HEAD_DIM],
                            preferred_element_type=F32)
        o = acc / jnp.sum(lrun, axis=-1, keepdims=True)
        for hh in range(GROUP):
            h = GROUP * g + hh
            a_ref[:, h * HEAD_DIM:(h + 1) * HEAD_DIM] = o[hh * td:(hh + 1) * td, :].astype(BF16)


def _sample_attention(page_table, q, bias, k_new, v_new, cache_kt, cache_vt, td, chunk_pages=4):
    bd, npages = page_table.shape
    page = cache_kt.shape[3]
    nk = bias.shape[1]
    rowb = lambda b, pt: (b, 0)
    grid_spec = pltpu.PrefetchScalarGridSpec(
        num_scalar_prefetch=1, grid=(bd,),
        in_specs=[
            pl.BlockSpec((td, ATTN_WIDTH), rowb),
            pl.BlockSpec((td, nk), rowb),
            pl.BlockSpec((td, KV_WIDTH), rowb),
            pl.BlockSpec((td, KV_WIDTH), rowb),
            pl.BlockSpec(memory_space=pl.ANY),
            pl.BlockSpec(memory_space=pl.ANY),
        ],
        out_specs=pl.BlockSpec((td, ATTN_WIDTH), rowb),
        scratch_shapes=[
            pltpu.VMEM((2, npages, N_KV_HEADS, HEAD_DIM, page), F32),
            pltpu.VMEM((2, npages, N_KV_HEADS, HEAD_DIM, page), F32),
            pltpu.VMEM((N_HEADS * td, nk), F32),
            pltpu.SemaphoreType.DMA((2,)),
            pltpu.SemaphoreType.DMA((2,)),
        ],
    )
    return pl.pallas_call(
        functools.partial(_sample_attn_kernel, chunk_pages=chunk_pages),
        grid_spec=grid_spec, out_shape=jax.ShapeDtypeStruct((bd * td, ATTN_WIDTH), BF16),
        compiler_params=_cparams(("arbitrary",)), name="sample_dsa",
    )(page_table, q, bias, k_new, v_new, cache_kt, cache_vt)


def _layer_weights(i, g_mix, w_in, g_idx_k, b_idx_k, w_dw, b_dw, g_conv_ln, b_conv_ln, w_out, g_ffn, w_up,
                   w_ffn_conv, b_ffn_conv, w_down, g_ple, w_ple_gate, w_ple, g_final):
    d = w_in.shape[1]
    w = w_in[i]
    n_qkvi = C_KI + IDX_DIM + IDX_HEADS
    pad = jnp.zeros((d, LANES - IDX_DIM - IDX_HEADS), w.dtype)
    w_comb = jnp.concatenate([w[:, :n_qkvi], pad, w[:, n_qkvi:]], axis=1).astype(BF16)
    zpad = jnp.zeros((LANES - IDX_DIM,), F32)
    gik = jnp.concatenate([g_idx_k[i], zpad])[None]
    bik = jnp.concatenate([b_idx_k[i], zpad])[None]
    proj = (g_mix[i][None], w_comb, gik, bik)

    wo = w_out[i].astype(BF16)
    conv = (w_dw[i], b_dw[i][None], g_conv_ln[i][None], b_conv_ln[i][None], wo[:ATTN_WIDTH], wo[ATTN_WIDTH:])

    d_ff = w_down.shape[1]
    nch = d_ff // FFN_CW
    ffn = (g_ffn[i][None], _ffn_chunks(w_up[i].astype(BF16), nch), _ffn_chunks(w_ffn_conv[i], nch),
           _ffn_chunks(b_ffn_conv[i][None], nch), w_down[i].astype(BF16).reshape(nch, FFN_CW, d),
           g_ple[i][None], w_ple_gate[i].astype(BF16), w_ple[i].astype(BF16), g_final[None])
    return proj, conv, ffn


def _ffn_chunks(m, nch):
    rows = m.shape[0]
    return m.reshape(rows, 2, nch, FFN_CW).transpose(2, 0, 1, 3).reshape(nch, rows, 2 * FFN_CW)


def _ffn_unchunk(st):
    nch, rows, cw2 = st.shape
    return st.reshape(nch, rows, 2, cw2 // 2).transpose(1, 2, 0, 3).reshape(rows, nch * cw2)


def kernel(x_prompt, x_sample, p_prompt, p_sample, cache_k, cache_v, cache_idx_k, state_conv, state_ffn_conv,
           page_table, g_mix, w_in, g_idx_k, b_idx_k, w_dw, b_dw, g_conv_ln, b_conv_ln, w_out, g_ffn, w_up,
           w_ffn_conv, b_ffn_conv, w_down, g_ple, w_ple_gate, w_ple, g_final):
    B, S, D = x_prompt.shape
    Bd, Td, _ = x_sample.shape
    depth = w_in.shape[0]
    n_pages = page_table.shape[1]
    page = cache_k.shape[2]
    past_len = n_pages * page
    d_ff = w_down.shape[1]
    nch = d_ff // FFN_CW
    topk_prompt = min(TOPK_MAX, S // 4)
    topk_sample = min(TOPK_MAX, (past_len + Td) // 4)
    assert S % KC == 0 and Td == SUBLANES and d_ff % FFN_CW == 0 and page == LANES

    tm_p = KC
    BL = min(32, Bd)
    G = Bd // BL
    tm_s = Td * BL

    tabs_p = _rope_tables(jnp.arange(S, dtype=I32))
    pos_s = past_len + jnp.arange(Td, dtype=I32)
    tabs_s = tuple(jnp.tile(t, (Bd, 1)) for t in _rope_tables(pos_s))

    def to_tm(a):
        return a.reshape(G, BL, Td, a.shape[-1]).transpose(0, 2, 1, 3).reshape(G * tm_s, a.shape[-1])

    def from_tm(a):
        return a.reshape(G, Td, BL, a.shape[-1]).transpose(0, 2, 1, 3).reshape(Bd, Td, a.shape[-1])

    hp = x_prompt.reshape(B * S, D)
    hs = x_sample.reshape(Bd * Td, D)
    outs = {k: [] for k in ("kp", "vp", "ip", "cp", "fp", "ks", "vs", "is", "cs", "fs")}
    for i in range(depth):
        last = i == depth - 1
        proj_w, conv_w, ffn_w = _layer_weights(
            i, g_mix, w_in, g_idx_k, b_idx_k, w_dw, b_dw, g_conv_ln, b_conv_ln, w_out, g_ffn, w_up,
            w_ffn_conv, b_ffn_conv, w_down, g_ple, w_ple_gate, w_ple, g_final)

        q, qit, kt, vt, kit, ktb, vb, kib, wit, glu = _project(
            hp, tabs_p, S // tm_p, *proj_w, tm=tm_p, prompt=True, batch=B)
        a = _prompt_attention(q, qit, wit, kib, ktb, vb, B, S, topk_prompt)
        nt = S // tm_p
        halo_rows = 32
        halo_spec = pl.BlockSpec(
            (halo_rows, CONV_CH), lambda b, t: (jnp.maximum((b * nt + t) * (tm_p // halo_rows) - 1, 0), 0))
        h1 = _conv_out(glu, glu, halo_spec, a, hp, *conv_w, groups=B, nt=nt, tm=tm_p, shift=1, zero_first=True)
        st0 = jnp.zeros((B, nch, SUBLANES, 2 * FFN_CW), F32)
        hp, st = _ffn(h1, p_prompt[i].reshape(B * S, -1), st0, ffn_w, groups=B, nt=nt, tm=tm_p,
                      shift=1, halo=SUBLANES, final=last)
        outs["kp"].append(kt.transpose(0, 3, 1, 2))
        outs["vp"].append(vt.transpose(0, 3, 1, 2))
        outs["ip"].append(kit.transpose(0, 2, 1))
        outs["cp"].append(glu.reshape(B, S, CONV_CH)[:, S - (CONV_W - 1):])
        outs["fp"].append(jax.vmap(_ffn_unchunk)(st)[:, SUBLANES - (FFN_CONV_W - 1):])

        tm_sp = min(512, Bd * Td)
        q, qi, k, v, ki, kib, wi, glu = _project(hs, tabs_s, Bd * Td // tm_sp, *proj_w, tm=tm_sp, prompt=False)
        keys = _sample_scores(page_table, qi, wi, kib, cache_idx_k[i].transpose(0, 2, 1), Td)
        bias = _select(keys, topk_sample, rows=min(64, Bd * Td))
        a = _sample_attention(page_table, q, bias, k, v,
                              cache_k[i].transpose(0, 2, 3, 1), cache_v[i].transpose(0, 2, 3, 1), Td)
        sc = state_conv[i]
        halo_s = sc.reshape(G, BL, CONV_W - 1, CONV_CH).transpose(0, 2, 1, 3).reshape(G * (CONV_W - 1) * BL, CONV_CH)
        halo_spec = pl.BlockSpec(((CONV_W - 1) * BL, CONV_CH), lambda g, t: (g, 0))
        h1 = _conv_out(to_tm(glu), halo_s, halo_spec, to_tm(a), to_tm(hs), *conv_w, groups=G, nt=1, tm=tm_s,
                       shift=BL, zero_first=False)
        sf = state_ffn_conv[i]
        sf = sf.reshape(G, BL, FFN_CONV_W - 1, 2 * d_ff).transpose(0, 2, 1, 3).reshape(G, (FFN_CONV_W - 1) * BL, 2 * d_ff)
        st0 = jax.vmap(lambda s: _ffn_chunks(s, nch))(sf)
        hs_tm, st = _ffn(h1, to_tm(p_sample[i].reshape(Bd * Td, -1)), st0, ffn_w, groups=G, nt=1, tm=tm_s,
                         shift=BL, halo=(FFN_CONV_W - 1) * BL, final=last)
        hs = from_tm(hs_tm).reshape(Bd * Td, D)
        fs = jax.vmap(_ffn_unchunk)(st)
        fs = fs.reshape(G, FFN_CONV_W - 1, BL, 2 * d_ff).transpose(0, 2, 1, 3).reshape(Bd, FFN_CONV_W - 1, 2 * d_ff)
        outs["ks"].append(k.reshape(Bd, Td, N_KV_HEADS, HEAD_DIM))
        outs["vs"].append(v.reshape(Bd, Td, N_KV_HEADS, HEAD_DIM))
        outs["is"].append(ki.reshape(Bd, Td, IDX_DIM))
        outs["cs"].append(jnp.concatenate([sc, glu.reshape(Bd, Td, CONV_CH)], axis=1)[:, Td:])
        outs["fs"].append(fs)

    y_prompt = hp.reshape(B, S, D)
    y_sample = hs.reshape(Bd, Td, D)
    st = lambda name: jnp.stack(outs[name])
    return (y_prompt, y_sample, st("kp"), st("vp"), st("ip"), st("cp"), st("fp"),
            st("ks"), st("vs"), st("is"), st("cs"), st("fs"))
```

```python
import functools

import jax
import jax.numpy as jnp
from jax import lax
from jax.experimental import pallas as pl
from jax.experimental.pallas import tpu as pltpu

N_HEADS = 8
HEAD_DIM = 64
N_KV_HEADS = 2
GROUP = N_HEADS // N_KV_HEADS
ATTN_WIDTH = N_HEADS * HEAD_DIM
KV_WIDTH = N_KV_HEADS * HEAD_DIM
IDX_HEADS = 8
IDX_DIM = 64
IDX_ROPE_DIM = 32
TOPK_MAX = 256
CONV_W = 31
FFN_CONV_W = 3
ROPE_THETA = 10000.0
EPS = 1e-6
NEG = -1e30

LANES = 128
SUBLANES = 8
INT_MIN = -2 ** 31
VMEM_LIMIT = 56 * 1024 * 1024

TQ = LANES
TPC = 4
KC = TPC * LANES
CPC = TPC
Q_SCALE = HEAD_DIM ** -0.5 * 1.4426950408889634
BISECT_FLOAT_PASSES = 16
BISECT_MIN_PASSES = 12
BISECT_CHECK_EVERY = 4
BISECT_MAX_PASSES = 48

F32 = jnp.float32
BF16 = jnp.bfloat16
I32 = jnp.int32

_NT = (((1,), (1,)), ((), ()))


def _cparams(sem):
    return pltpu.CompilerParams(dimension_semantics=sem, vmem_limit_bytes=VMEM_LIMIT)


def _sortable(x):
    b = pltpu.bitcast(x, I32)
    return b ^ ((b >> 31) & jnp.int32(0x7FFFFFFF))


def _rmsnorm(x, g):
    return x * lax.rsqrt(jnp.mean(x * x, axis=-1, keepdims=True) + EPS) * g


C_Q = 0
C_K = C_Q + ATTN_WIDTH
C_V = C_K + KV_WIDTH
C_QI = C_V + KV_WIDTH
C_KI = C_QI + IDX_HEADS * IDX_DIM
C_A = C_KI + LANES
CONV_CH = 512
C_G = C_A + CONV_CH
C_END = C_G + CONV_CH


def _proj_kernel(x_ref, g_ref, w_ref, gik_ref, bik_ref, cq_ref, sq_ref, ci_ref, si_ref, *out_refs,
                 wi_scale, prompt):
    tm = x_ref.shape[0]
    hn = _rmsnorm(x_ref[...], g_ref[...]).astype(BF16)
    z = jnp.dot(hn, w_ref[...], preferred_element_type=F32)

    lane = lax.broadcasted_iota(I32, (tm, LANES), 1)
    in_head = lane % HEAD_DIM
    cq, sq, ci, si = cq_ref[...], sq_ref[...], ci_ref[...], si_ref[...]

    def rope_full(xg):
        sw = jnp.where(in_head < HEAD_DIM // 2, pltpu.roll(xg, LANES - HEAD_DIM // 2, 1),
                       pltpu.roll(xg, HEAD_DIM // 2, 1))
        return xg * cq + sw * sq

    def rope_part(xg):
        sw = jnp.where(in_head < IDX_ROPE_DIM // 2, pltpu.roll(xg, LANES - IDX_ROPE_DIM // 2, 1),
                       pltpu.roll(xg, IDX_ROPE_DIM // 2, 1))
        return xg * ci + sw * si

    q_groups = [rope_full(z[:, C_Q + g * LANES:C_Q + (g + 1) * LANES]) * Q_SCALE
                for g in range(ATTN_WIDTH // LANES)]
    qi_groups = [rope_part(z[:, C_QI + g * LANES:C_QI + (g + 1) * LANES])
                 for g in range(IDX_HEADS * IDX_DIM // LANES)]
    kr = rope_full(z[:, C_K:C_K + LANES])
    vr = z[:, C_V:C_V + LANES]

    zg = z[:, C_KI:C_KI + LANES]
    is_ki = lane < IDX_DIM
    mu = jnp.sum(jnp.where(is_ki, zg, 0.0), axis=-1, keepdims=True) / IDX_DIM
    xc = jnp.where(is_ki, zg - mu, 0.0)
    var = jnp.sum(xc * xc, axis=-1, keepdims=True) / IDX_DIM
    kin = rope_part(xc * lax.rsqrt(var + EPS) * gik_ref[...] + bik_ref[...])
    wig = zg * wi_scale
    glu = z[:, C_A:C_A + CONV_CH] * jax.nn.sigmoid(z[:, C_G:C_G + CONV_CH])

    if prompt:
        q_ref, qit_ref, kt_ref, vt_ref, kit_ref, ktb_ref, vb_ref, kib_ref, wit_ref, glu_ref = out_refs
        for g, qg in enumerate(q_groups):
            q_ref[:, g * LANES:(g + 1) * LANES] = qg.astype(BF16)
        for g, qig in enumerate(qi_groups):
            qit_ref[g * LANES:(g + 1) * LANES, :] = qig.T.astype(BF16)
        krt = kr.T
        vrt = vr.T
        for g in range(N_KV_HEADS):
            rows = slice(g * HEAD_DIM, (g + 1) * HEAD_DIM)
            kt_ref[0, g] = krt[rows]
            vt_ref[0, g] = vrt[rows]
            ktb_ref[0, g] = krt[rows].astype(BF16)
            vg = vr if g == 0 else pltpu.roll(vr, LANES - g * HEAD_DIM, 1)
            vb_ref[g] = jnp.where(lane < HEAD_DIM, vg, jnp.where(lane == HEAD_DIM, 1.0, 0.0)).astype(BF16)
        kit_ref[0] = kin.T[:IDX_DIM]
        kib_ref[...] = kin[:, :IDX_DIM].astype(BF16)
        wit_ref[...] = wig.T[IDX_DIM:IDX_DIM + IDX_HEADS, :]
        glu_ref[...] = glu
    else:
        q_ref, qi_ref, k_ref, v_ref, ki_ref, kib_ref, wi_ref, glu_ref = out_refs
        for g, qg in enumerate(q_groups):
            q_ref[:, g * LANES:(g + 1) * LANES] = qg.astype(BF16)
        for g, qig in enumerate(qi_groups):
            qi_ref[:, g * LANES:(g + 1) * LANES] = qig.astype(BF16)
        k_ref[...] = kr
        v_ref[...] = vr
        ki_ref[...] = kin[:, :IDX_DIM]
        kib_ref[...] = kin[:, :IDX_DIM].astype(BF16)
        wi_ref[...] = wig[:, IDX_DIM:IDX_DIM + IDX_HEADS]
        glu_ref[...] = glu


def _rope_tables(pos):
    def cs(half):
        inv = jnp.power(jnp.float32(ROPE_THETA), -jnp.arange(half, dtype=F32) / half)
        ang = pos.astype(F32)[:, None] * inv[None, :]
        return jnp.cos(ang), jnp.sin(ang)

    c, s = cs(HEAD_DIM // 2)
    cq = jnp.concatenate([c, c, c, c], axis=-1)
    sq = jnp.concatenate([-s, s, -s, s], axis=-1)
    c2, s2 = cs(IDX_ROPE_DIM // 2)
    one = jnp.ones((pos.shape[0], IDX_DIM - IDX_ROPE_DIM), F32)
    ci = jnp.concatenate([c2, c2, one, c2, c2, one], axis=-1)
    si = jnp.concatenate([-s2, s2, 0 * one, -s2, s2, 0 * one], axis=-1)
    return cq, sq, ci, si


def _project(x, tables, tab_blocks, g_mix, w_comb, gik, bik, *, tm, prompt, batch=None):
    n, d = x.shape
    nb = n // tm
    row = lambda i: (i, 0)
    const = lambda i: (0, 0)
    tab = lambda i: (i % tab_blocks, 0)
    wi_scale = IDX_HEADS ** -0.5 * IDX_DIM ** -0.5
    sds = jax.ShapeDtypeStruct
    if prompt:
        seq = n // batch
        nt = seq // tm
        bt = lambda i: (i // nt, 0, 0, i % nt)
        out_shape = (
            sds((n, ATTN_WIDTH), BF16),
            sds((IDX_HEADS * IDX_DIM, n), BF16),
            sds((batch, N_KV_HEADS, HEAD_DIM, seq), F32),
            sds((batch, N_KV_HEADS, HEAD_DIM, seq), F32),
            sds((batch, IDX_DIM, seq), F32),
            sds((nb, N_KV_HEADS, HEAD_DIM, tm), BF16),
            sds((N_KV_HEADS, n, LANES), BF16),
            sds((n, IDX_DIM), BF16),
            sds((IDX_HEADS, n), F32),
            sds((n, CONV_CH), F32),
        )
        out_specs = (
            pl.BlockSpec((tm, ATTN_WIDTH), row),
            pl.BlockSpec((IDX_HEADS * IDX_DIM, tm), lambda i: (0, i)),
            pl.BlockSpec((1, N_KV_HEADS, HEAD_DIM, tm), bt),
            pl.BlockSpec((1, N_KV_HEADS, HEAD_DIM, tm), bt),
            pl.BlockSpec((1, IDX_DIM, tm), lambda i: (i // nt, 0, i % nt)),
            pl.BlockSpec((1, N_KV_HEADS, HEAD_DIM, tm), lambda i: (i, 0, 0, 0)),
            pl.BlockSpec((N_KV_HEADS, tm, LANES), lambda i: (0, i, 0)),
            pl.BlockSpec((tm, IDX_DIM), row),
            pl.BlockSpec((IDX_HEADS, tm), lambda i: (0, i)),
            pl.BlockSpec((tm, CONV_CH), row),
        )
    else:
        out_shape = (
            sds((n, ATTN_WIDTH), BF16),
            sds((n, IDX_HEADS * IDX_DIM), BF16),
            sds((n, KV_WIDTH), F32),
            sds((n, KV_WIDTH), F32),
            sds((n, IDX_DIM), F32),
            sds((n, IDX_DIM), BF16),
            sds((n, IDX_HEADS), F32),
            sds((n, CONV_CH), F32),
        )
        out_specs = (
            pl.BlockSpec((tm, ATTN_WIDTH), row),
            pl.BlockSpec((tm, IDX_HEADS * IDX_DIM), row),
            pl.BlockSpec((tm, KV_WIDTH), row),
            pl.BlockSpec((tm, KV_WIDTH), row),
            pl.BlockSpec((tm, IDX_DIM), row),
            pl.BlockSpec((tm, IDX_DIM), row),
            pl.BlockSpec((tm, IDX_HEADS), row),
            pl.BlockSpec((tm, CONV_CH), row),
        )
    in_specs = [
        pl.BlockSpec((tm, d), row),
        pl.BlockSpec((1, d), const),
        pl.BlockSpec((d, C_END), const),
        pl.BlockSpec((1, LANES), const),
        pl.BlockSpec((1, LANES), const),
    ] + [pl.BlockSpec((tm, LANES), tab)] * 4
    return pl.pallas_call(
        functools.partial(_proj_kernel, wi_scale=wi_scale, prompt=prompt),
        grid=(nb,), in_specs=in_specs, out_specs=out_specs, out_shape=out_shape,
        compiler_params=_cparams(("parallel",)), name="in_proj",
    )(x, g_mix, w_comb, gik, bik, *tables)


def _tile_count(pred):
    return jnp.sum(pred.astype(I32).reshape(LANES // SUBLANES, SUBLANES, LANES), axis=0)


def _count_keys(key_ref, nsteps, pred):
    def body(c, accs):
        return tuple(a + _tile_count(pred(key_ref[c * CPC + j], c * CPC + j)) for j, a in enumerate(accs))
    z8 = jnp.zeros((SUBLANES, LANES), I32)
    accs = lax.fori_loop(0, nsteps, body, (z8,) * CPC)
    return jnp.sum(sum(accs[1:], accs[0]), axis=0, keepdims=True)


def _prompt_attn_kernel(qit_ref, wit_ref, q_ref, kib_ref, ktb_ref, vb_ref, a_ref,
                        key_ref, bias_ref, s_ref, m_ref, acc_ref, *, topk):
    i = pl.program_id(1)
    nch = i // TPC + 1
    r_io = lax.broadcasted_iota(I32, (TQ, TQ), 0)
    c_io = lax.broadcasted_iota(I32, (TQ, TQ), 1)

    def causal(kt):
        return r_io + (kt - i) * TQ <= c_io

    qit = jnp.concatenate([qit_ref[h * IDX_DIM:(h + 1) * IDX_DIM, :] for h in range(IDX_HEADS)], axis=1)
    w_rows = [wit_ref[h:h + 1, :] for h in range(IDX_HEADS)]

    def score_body(c, mm):
        kmin, kmax = mm
        for j in range(TPC):
            kt = c * TPC + j
            kit = kib_ref[pl.ds(pl.multiple_of(kt * TQ, TQ), TQ), :]
            lg = jnp.dot(kit, qit, preferred_element_type=F32)
            sc = jnp.maximum(lg[:, :TQ], 0.0) * w_rows[0]
            for h in range(1, IDX_HEADS):
                sc = sc + jnp.maximum(lg[:, h * TQ:(h + 1) * TQ], 0.0) * w_rows[h]
            vis = causal(kt)
            k = _sortable(sc)
            key_ref[kt] = jnp.where(vis, k, INT_MIN)
            kmax = jnp.maximum(kmax, jnp.where(vis, k, INT_MIN))
            kmin = jnp.minimum(kmin, jnp.where(vis, k, jnp.int32(2 ** 31 - 1)))
        return kmin, kmax
    kmin, kmax = lax.fori_loop(0, nch, score_body, (jnp.full((TQ, TQ), 2 ** 31 - 1, I32),
                                                    jnp.full((TQ, TQ), INT_MIN, I32)))

    n_keys = i * TQ + c_io[0:1, :] + 1
    enough = n_keys >= topk
    lo0 = jnp.min(kmin, axis=0, keepdims=True)
    hi0 = jnp.max(kmax, axis=0, keepdims=True) + 1

    def unsort(k):
        return pltpu.bitcast(k ^ ((k >> 31) & jnp.int32(0x7FFFFFFF)), F32)

    def bisect_pass(p, lo, hi, cnt):
        mid_i = (lo >> 1) + (hi >> 1) + (lo & hi & 1)
        mid_f = _sortable(0.5 * unsort(lo) + 0.5 * unsort(hi))
        use_f = (p < BISECT_FLOAT_PASSES) & (mid_f > lo) & (mid_f < hi)
        mid = jnp.where(use_f, mid_f, mid_i)
        c = _count_keys(key_ref, nch, lambda k, kt: k >= mid)
        ge = c >= topk
        return jnp.where(ge, mid, lo), jnp.where(ge, hi, mid), jnp.where(ge, c, cnt)

    def bisect_cond(st):
        p, done = st[0], st[1]
        return jnp.logical_and(jnp.logical_not(done), p < BISECT_MAX_PASSES)

    def bisect_body(st):
        p, _, lo, hi, cnt = st
        for u in range(BISECT_CHECK_EVERY):
            lo, hi, cnt = bisect_pass(p + u, lo, hi, cnt)
        settled = (cnt == topk) | (lo + 1 >= hi) | jnp.logical_not(enough)
        return p + BISECT_CHECK_EVERY, jnp.min(settled.astype(I32)) > 0, lo, hi, cnt

    lo1, hi1, cnt1 = lax.fori_loop(0, BISECT_MIN_PASSES, lambda p, st: bisect_pass(p, *st),
                                   (lo0, hi0, n_keys))
    _, _, lo, _, _ = lax.while_loop(bisect_cond, bisect_body,
                                    (jnp.int32(BISECT_MIN_PASSES), jnp.bool_(False), lo1, hi1, cnt1))
    thr = jnp.where(enough, lo, INT_MIN)

    n_gt = _count_keys(key_ref, nch, lambda k, kt: k > thr)
    n_eq = _count_keys(key_ref, nch, lambda k, kt: k == thr)
    need = topk - n_gt
    n_idx_bits = max(1, (key_ref.shape[0] * TQ).bit_length())

    def tie_search(_):
        def body(p, m):
            cand = m | jnp.left_shift(jnp.int32(1), n_idx_bits - 1 - p)
            below = _count_keys(key_ref, nch, lambda k, kt: (k == thr) & (r_io + kt * TQ < cand))
            return jnp.where(below < need, cand, m)
        return lax.fori_loop(0, n_idx_bits, body, jnp.zeros((1, TQ), I32))

    excess = jnp.max(jnp.where((need > 0) & (thr > INT_MIN), n_eq - need, 0)) > 0
    idx_max = lax.cond(excess, tie_search,
                       lambda _: jnp.full((1, TQ), key_ref.shape[0] * TQ, I32), 0)

    def write_bias(c):
        for j in range(TPC):
            kt = c * TPC + j
            k = key_ref[kt]
            sel = ((k > thr) | ((k == thr) & (r_io + kt * TQ <= idx_max))) & causal(kt)
            bias_ref[kt] = jnp.where(sel, 0.0, NEG).astype(F32).T.astype(BF16)
    write_bias(0)

    q = q_ref[...]
    row_q = lax.broadcasted_iota(I32, (GROUP * TQ, TQ), 0) % TQ
    onehot = (row_q == lax.broadcasted_iota(I32, (GROUP * TQ, TQ), 1)).astype(BF16)
    lhs = [jnp.concatenate([onehot] + [jnp.concatenate(
        [q[:, (GROUP * g + hh) * HEAD_DIM:(GROUP * g + hh + 1) * HEAD_DIM] for hh in range(GROUP)], axis=0)],
        axis=1) for g in range(N_KV_HEADS)]
    m_ref[...] = jnp.full(m_ref.shape, NEG, F32)

    def qk_body(c, carry):
        bias = jnp.concatenate([bias_ref[c * TPC + j] for j in range(TPC)], axis=1)
        for g in range(N_KV_HEADS):
            rhs = jnp.concatenate([bias, ktb_ref[c, g]], axis=0)
            s = jnp.dot(lhs[g], rhs, preferred_element_type=F32)
            s_ref[g, c] = s
            cm = s[:, :TQ]
            for j in range(1, TPC):
                cm = jnp.maximum(cm, s[:, j * TQ:(j + 1) * TQ])
            m_ref[g] = jnp.maximum(m_ref[g], cm)
        write_bias(jnp.minimum(c + 1, nch - 1))
        return carry
    lax.fori_loop(0, nch, qk_body, 0)
    m = [jnp.max(m_ref[g], axis=-1, keepdims=True) for g in range(N_KV_HEADS)]
    acc_ref[...] = jnp.zeros(acc_ref.shape, F32)

    def pv_body(c, carry):
        for g in range(N_KV_HEADS):
            p = jnp.exp2(s_ref[g, c] - m[g]).astype(BF16)
            vc = vb_ref[g, pl.ds(pl.multiple_of(c * KC, KC), KC), :]
            acc_ref[g] += jnp.dot(p, vc, preferred_element_type=F32)
        return carry
    lax.fori_loop(0, nch, pv_body, 0)
    for g in range(N_KV_HEADS):
        acc = acc_ref[g]
        o = acc[:, :HEAD_DIM] / acc[:, HEAD_DIM:HEAD_DIM + 1]
        for hh in range(GROUP):
            h = GROUP * g + hh
            a_ref[:, h * HEAD_DIM:(h + 1) * HEAD_DIM] = o[hh * TQ:(hh + 1) * TQ, :].astype(BF16)


def _prompt_attention(q, qit, wit, kib, ktb, vb, batch, seq, topk):
    nb = seq // TQ
    ncs = seq // KC
    qrow = lambda b, i: (b * nb + i, 0)
    qcol = lambda b, i: (0, b * nb + i)
    return pl.pallas_call(
        functools.partial(_prompt_attn_kernel, topk=topk),
        grid=(batch, nb),
        in_specs=[
            pl.BlockSpec((IDX_HEADS * IDX_DIM, TQ), qcol),
            pl.BlockSpec((IDX_HEADS, TQ), qcol),
            pl.BlockSpec((TQ, ATTN_WIDTH), qrow),
            pl.BlockSpec((seq, IDX_DIM), lambda b, i: (b, 0)),
            pl.BlockSpec((ncs, N_KV_HEADS, HEAD_DIM, KC), lambda b, i: (b, 0, 0, 0)),
            pl.BlockSpec((N_KV_HEADS, seq, LANES), lambda b, i: (0, b, 0)),
        ],
        out_specs=pl.BlockSpec((TQ, ATTN_WIDTH), qrow),
        out_shape=jax.ShapeDtypeStruct((batch * seq, ATTN_WIDTH), BF16),
        scratch_shapes=[
            pltpu.VMEM((nb, TQ, TQ), I32),
            pltpu.VMEM((nb, TQ, TQ), BF16),
            pltpu.VMEM((N_KV_HEADS, ncs, GROUP * TQ, KC), F32),
            pltpu.VMEM((N_KV_HEADS, GROUP * TQ, TQ), F32),
            pltpu.VMEM((N_KV_HEADS, GROUP * TQ, LANES), F32),
        ],
        compiler_params=_cparams(("parallel", "arbitrary")), name="prompt_dsa",
    )(qit, wit, q, kib, ktb, vb)


def _conv_out_kernel(glu_ref, halo_ref, a_ref, x_ref, wdw_ref, bdw_ref, gln_ref, bln_ref, wa_ref, wc_ref,
                     h_ref, xp_ref, c_ref, *maybe_xs_ref, shift, zero_first, rc):
    tm = glu_ref.shape[0]
    halo = halo_ref.shape[0]
    hv = halo_ref[...]
    if zero_first:
        hv = jnp.where(pl.program_id(1) == 0, 0.0, hv)
    xp_ref[0:halo, :] = hv
    xp_ref[halo:halo + tm, :] = glu_ref[...]
    if shift % SUBLANES:
        (xs_ref,) = maybe_xs_ref
        for r in range(1, SUBLANES):
            xs_ref[r - 1] = xp_ref[r:r + xs_ref.shape[1], :]

    def window(off, ls):
        r = off % SUBLANES
        if r == 0:
            return xp_ref[off:off + rc, ls]
        return xs_ref[r - 1, off - r:off - r + rc, ls]

    for lg in range(CONV_CH // LANES):
        ls = slice(lg * LANES, (lg + 1) * LANES)
        for r0 in range(0, tm, rc):
            acc = jnp.zeros((rc, LANES), F32)
            for j in range(CONV_W):
                off = halo - (CONV_W - 1 - j) * shift
                acc = acc + wdw_ref[j:j + 1, ls] * window(r0 + off, ls)
            c_ref[r0:r0 + rc, ls] = acc + bdw_ref[:, ls]

    c = c_ref[...]
    mu = jnp.mean(c, axis=-1, keepdims=True)
    xc = c - mu
    var = jnp.mean(xc * xc, axis=-1, keepdims=True)
    y = xc * lax.rsqrt(var + EPS) * gln_ref[...] + bln_ref[...]
    y = jax.nn.silu(y)
    h_ref[...] = (x_ref[...]
                  + jnp.dot(a_ref[...], wa_ref[...], preferred_element_type=F32)
                  + jnp.dot(y.astype(BF16), wc_ref[...], preferred_element_type=F32))


def _conv_out(glu, halo_arr, halo_spec, a, x, w_dw, b_dw, g_ln, b_ln, w_a, w_c, *, groups, nt, tm,
              shift, zero_first):
    d = x.shape[1]
    halo = halo_spec.block_shape[0]
    row = lambda b, i: (b * nt + i, 0)
    const = lambda b, i: (0, 0)
    return pl.pallas_call(
        functools.partial(_conv_out_kernel, shift=shift, zero_first=zero_first, rc=32),
        grid=(groups, nt),
        in_specs=[
            pl.BlockSpec((tm, CONV_CH), row),
            halo_spec,
            pl.BlockSpec((tm, ATTN_WIDTH), row),
            pl.BlockSpec((tm, d), row),
            pl.BlockSpec((CONV_W, CONV_CH), const),
            pl.BlockSpec((1, CONV_CH), const),
            pl.BlockSpec((1, CONV_CH), const),
            pl.BlockSpec((1, CONV_CH), const),
            pl.BlockSpec((ATTN_WIDTH, d), const),
            pl.BlockSpec((CONV_CH, d), const),
        ],
        out_specs=pl.BlockSpec((tm, d), row),
        out_shape=jax.ShapeDtypeStruct(x.shape, F32),
        scratch_shapes=[pltpu.VMEM((halo + tm, CONV_CH), F32), pltpu.VMEM((tm, CONV_CH), F32)]
        + ([pltpu.VMEM((SUBLANES - 1, halo + tm - SUBLANES, CONV_CH), F32)] if shift % SUBLANES else []),
        compiler_params=_cparams(("parallel", "arbitrary")), name="conv_out_proj",
    )(glu, halo_arr, a, x, w_dw, b_dw, g_ln, b_ln, w_a, w_c)


FFN_CW = 256
FFN_RC = 32


def _ffn_kernel(h_ref, p_ref, st_ref, gffn_ref, wu_ref, wf_ref, bf_ref, wdn_ref, gple_ref, wgate_ref, wple_ref,
                gfin_ref, y_ref, sto_ref, u_ref, act_ref, carry_ref, *, shift, halo, final):
    tm = h_ref.shape[0]
    nch = wu_ref.shape[0]
    cw = wu_ref.shape[2] // 2
    first = pl.program_id(1) == 0
    h = h_ref[...]
    hn = _rmsnorm(h, gffn_ref[...]).astype(BF16)

    @pl.when(first)
    def _():
        carry_ref[...] = st_ref[...]

    def up(c, buf):
        ub = u_ref.at[buf]
        ub[0:halo, :] = carry_ref[c]
        ub[halo:halo + tm, :] = jnp.dot(hn, wu_ref[c], preferred_element_type=F32)
        tail = ub[tm:tm + halo, :]
        carry_ref[c] = tail
        sto_ref[c] = tail

    def act(c, buf):
        wf = wf_ref[c]
        bf = bf_ref[c]
        for r0 in range(0, tm, FFN_RC):
            out = bf
            for k in range(FFN_CONV_W):
                off = r0 + halo - (FFN_CONV_W - 1 - k) * shift
                out = out + wf[k:k + 1, :] * u_ref[buf, off:off + FFN_RC, :]
            act_ref[c, r0:r0 + FFN_RC, :] = (jax.nn.silu(out[:, :cw]) * out[:, cw:]).astype(BF16)

    up(0, 0)
    n_pairs = (nch - 1) // 2

    def pair(k, carry):
        up(2 * k + 1, 1)
        act(2 * k, 0)
        up(2 * k + 2, 0)
        act(2 * k + 1, 1)
        return carry
    lax.fori_loop(0, n_pairs, pair, 0)
    if (nch - 1) % 2:
        up(nch - 1, 1)
        act(nch - 2, 0)
        act(nch - 1, 1)
    else:
        act(nch - 1, 0)

    down = jnp.dot(act_ref[0], wdn_ref[0], preferred_element_type=F32)
    for c in range(1, nch):
        down = down + jnp.dot(act_ref[c], wdn_ref[c], preferred_element_type=F32)
    h2 = h + down
    gate = jax.nn.sigmoid(jnp.dot(_rmsnorm(h2, gple_ref[...]).astype(BF16), wgate_ref[...],
                                  preferred_element_type=F32))
    ple = jnp.dot(p_ref[...].astype(BF16), wple_ref[...], preferred_element_type=F32)
    h3 = h2 + ple * gate
    y_ref[...] = _rmsnorm(h3, gfin_ref[...]) if final else h3


def _ffn(h, p, st, wts, *, groups, nt, tm, shift, halo, final):
    (g_ffn, wu, wf, bf, wdn, g_ple, w_gate, w_ple, g_fin) = wts
    d = h.shape[1]
    nch, _, cw2 = wu.shape
    row = lambda b, i: (b * nt + i, 0)
    c2 = lambda b, i: (0, 0)
    c3 = lambda b, i: (0, 0, 0)
    stspec = pl.BlockSpec((None, nch, halo, cw2), lambda b, i: (b, 0, 0, 0))
    once = dict(pipeline_mode=pl.Buffered(1))
    return pl.pallas_call(
        functools.partial(_ffn_kernel, shift=shift, halo=halo, final=final),
        grid=(groups, nt),
        in_specs=[
            pl.BlockSpec((tm, d), row),
            pl.BlockSpec((tm, p.shape[1]), row),
            stspec,
            pl.BlockSpec((1, d), c2),
            pl.BlockSpec(wu.shape, c3, **once),
            pl.BlockSpec(wf.shape, c3),
            pl.BlockSpec(bf.shape, c3),
            pl.BlockSpec(wdn.shape, c3, **once),
            pl.BlockSpec((1, d), c2),
            pl.BlockSpec(w_gate.shape, c2, **once),
            pl.BlockSpec(w_ple.shape, c2, **once),
            pl.BlockSpec((1, d), c2),
        ],
        out_specs=(pl.BlockSpec((tm, d), row), stspec),
        out_shape=(jax.ShapeDtypeStruct(h.shape, F32),
                   jax.ShapeDtypeStruct((groups, nch, halo, cw2), F32)),
        scratch_shapes=[
            pltpu.VMEM((2, halo + tm, cw2), F32),
            pltpu.VMEM((nch, tm, cw2 // 2), BF16),
            pltpu.VMEM((nch, halo, cw2), F32),
        ],
        compiler_params=_cparams(("parallel", "arbitrary")), name="conv_ffn_ple",
    )(h, p, st, g_ffn, wu, wf, bf, wdn, g_ple, w_gate, w_ple, g_fin)


def _start_pages(pt_ref, b, src_hbm, dst_ref, slot, sem):
    for p in range(dst_ref.shape[1]):
        pltpu.make_async_copy(src_hbm.at[pt_ref[b, p]], dst_ref.at[slot, p], sem.at[slot]).start()


def _wait_pages(src_hbm, dst_ref, slot, sem):
    for p in range(dst_ref.shape[1]):
        pltpu.make_async_copy(src_hbm.at[0], dst_ref.at[slot, p], sem.at[slot]).wait()


def _rows_by_head(x, width):
    return jnp.concatenate([x[:, h * width:(h + 1) * width] for h in range(x.shape[1] // width)], axis=0)


def _pad_rows(x, rows):
    return jnp.concatenate([x, jnp.zeros((rows - x.shape[0], x.shape[1]), x.dtype)], axis=0)


def _sample_score_kernel(pt_ref, qi_ref, wi_ref, kin_ref, cidx_hbm, key_ref, ibuf, sem, *, chunk_pages):
    b = pl.program_id(0)
    nb = pl.num_programs(0)
    slot = b % 2
    npages, page = ibuf.shape[1], ibuf.shape[3]
    td = qi_ref.shape[0]

    @pl.when(b == 0)
    def _():
        _start_pages(pt_ref, b, cidx_hbm, ibuf, 0, sem)

    @pl.when(b + 1 < nb)
    def _():
        _start_pages(pt_ref, b + 1, cidx_hbm, ibuf, 1 - slot, sem)

    qi = _rows_by_head(qi_ref[...], IDX_DIM)
    wcol = jnp.concatenate([wi_ref[:, h:h + 1] for h in range(IDX_HEADS)], axis=0)

    def head_sum(lg):
        r = jnp.maximum(lg, 0.0) * wcol
        return jnp.sum(r.reshape(IDX_HEADS, td, r.shape[1]), axis=0)

    _wait_pages(cidx_hbm, ibuf, slot, sem)
    cw = chunk_pages * page
    for c in range(npages // chunk_pages):
        kt = jnp.concatenate([ibuf[slot, c * chunk_pages + j] for j in range(chunk_pages)], axis=1)
        lg = jnp.dot(qi, kt.astype(BF16), preferred_element_type=F32)
        key_ref[:, c * cw:(c + 1) * cw] = _sortable(head_sum(lg))

    lg = lax.dot_general(qi, _pad_rows(kin_ref[...], LANES), _NT, preferred_element_type=F32)
    sc = head_sum(lg)
    t_io = lax.broadcasted_iota(I32, sc.shape, 0)
    j_io = lax.broadcasted_iota(I32, sc.shape, 1)
    key_ref[:, npages * page:] = jnp.where(j_io <= t_io, _sortable(sc), INT_MIN)


def _sample_scores(page_table, qi, wi, kib, cache_idx_t, td, chunk_pages=4):
    bd, npages = page_table.shape
    page = cache_idx_t.shape[2]
    nk = npages * page + LANES
    grid_spec = pltpu.PrefetchScalarGridSpec(
        num_scalar_prefetch=1, grid=(bd,),
        in_specs=[
            pl.BlockSpec((td, IDX_HEADS * IDX_DIM), lambda b, pt: (b, 0)),
            pl.BlockSpec((td, IDX_HEADS), lambda b, pt: (b, 0)),
            pl.BlockSpec((td, IDX_DIM), lambda b, pt: (b, 0)),
            pl.BlockSpec(memory_space=pl.ANY),
        ],
        out_specs=pl.BlockSpec((td, nk), lambda b, pt: (b, 0)),
        scratch_shapes=[pltpu.VMEM((2, npages, IDX_DIM, page), F32), pltpu.SemaphoreType.DMA((2,))],
    )
    return pl.pallas_call(
        functools.partial(_sample_score_kernel, chunk_pages=chunk_pages),
        grid_spec=grid_spec, out_shape=jax.ShapeDtypeStruct((bd * td, nk), I32),
        compiler_params=_cparams(("arbitrary",)), name="sample_scores",
    )(page_table, qi, wi, kib, cache_idx_t)


def _select_kernel(key_ref, bias_ref, *, topk):
    rows, nk = key_ref.shape
    ntile = nk // LANES

    def count(pred):
        acc = jnp.zeros((rows, LANES), I32)
        for c in range(ntile):
            acc = acc + pred(key_ref[:, c * LANES:(c + 1) * LANES], c).astype(I32)
        return jnp.sum(acc, axis=-1, keepdims=True)

    t0 = jnp.where(count(lambda k, c: k >= 0) >= topk, 0, INT_MIN).astype(I32)

    def bit_body(p, t):
        cand = t | jnp.left_shift(jnp.int32(1), 30 - p)
        return jnp.where(count(lambda k, c: k >= cand) >= topk, cand, t)
    thr = lax.fori_loop(0, 31, bit_body, t0)

    n_gt = count(lambda k, c: k > thr)
    n_eq = count(lambda k, c: k == thr)
    need = topk - n_gt
    l_io = lax.broadcasted_iota(I32, (rows, LANES), 1)
    n_idx_bits = max(1, nk.bit_length())

    def tie_search(_):
        def body(p, m):
            cand = m | jnp.left_shift(jnp.int32(1), n_idx_bits - 1 - p)
            below = count(lambda k, c: (k == thr) & (l_io + c * LANES < cand))
            return jnp.where(below < need, cand, m)
        return lax.fori_loop(0, n_idx_bits, body, jnp.zeros((rows, 1), I32))

    excess = jnp.max(jnp.where((need > 0) & (thr > INT_MIN), n_eq - need, 0)) > 0
    idx_max = lax.cond(excess, tie_search, lambda _: jnp.full((rows, 1), nk, I32), 0)

    for c in range(ntile):
        k = key_ref[:, c * LANES:(c + 1) * LANES]
        sel = ((k > thr) | ((k == thr) & (l_io + c * LANES <= idx_max))) & (k != INT_MIN)
        bias_ref[:, c * LANES:(c + 1) * LANES] = jnp.where(sel, 0.0, NEG).astype(F32)


def _select(keys, topk, rows):
    n, nk = keys.shape
    return pl.pallas_call(
        functools.partial(_select_kernel, topk=topk),
        grid=(n // rows,),
        in_specs=[pl.BlockSpec((rows, nk), lambda i: (i, 0))],
        out_specs=pl.BlockSpec((rows, nk), lambda i: (i, 0)),
        out_shape=jax.ShapeDtypeStruct((n, nk), F32),
        compiler_params=_cparams(("parallel",)), name="sample_select",
    )(keys)


def _sample_attn_kernel(pt_ref, q_ref, bias_ref, kn_ref, vn_ref, ck_hbm, cv_hbm, a_ref,
                        kbuf, vbuf, s_ref, ksem, vsem, *, chunk_pages):
    b = pl.program_id(0)
    nb = pl.num_programs(0)
    slot = b % 2
    npages, page = kbuf.shape[1], kbuf.shape[4]
    td = q_ref.shape[0]
    half = GROUP * td

    @pl.when(b == 0)
    def _():
        _start_pages(pt_ref, b, ck_hbm, kbuf, 0, ksem)
        _start_pages(pt_ref, b, cv_hbm, vbuf, 0, vsem)

    @pl.when(b + 1 < nb)
    def _():
        _start_pages(pt_ref, b + 1, ck_hbm, kbuf, 1 - slot, ksem)
        _start_pages(pt_ref, b + 1, cv_hbm, vbuf, 1 - slot, vsem)

    q = _rows_by_head(q_ref[...], HEAD_DIM)
    qg = [q[g * half:(g + 1) * half] for g in range(N_KV_HEADS)]
    cw = chunk_pages * page
    nchunk = npages // chunk_pages
    new_lo = npages * page
    kn = _pad_rows(kn_ref[...], LANES).astype(BF16)
    vn = _pad_rows(vn_ref[...], LANES).astype(BF16)

    def chunk_t(buf, c, g):
        return jnp.concatenate([buf[slot, c * chunk_pages + j, g] for j in range(chunk_pages)],
                               axis=1).astype(BF16)

    def tile_bias(lo, n):
        return jnp.concatenate([bias_ref[:, lo:lo + n]] * GROUP, axis=0)

    _wait_pages(ck_hbm, kbuf, slot, ksem)
    ms = []
    for g in range(N_KV_HEADS):
        rows = slice(g * half, (g + 1) * half)
        mrun = jnp.full((half, LANES), NEG, F32)
        for c in range(nchunk):
            s = jnp.dot(qg[g], chunk_t(kbuf, c, g), preferred_element_type=F32) + tile_bias(c * cw, cw)
            s_ref[rows, c * cw:(c + 1) * cw] = s
            for j in range(cw // LANES):
                mrun = jnp.maximum(mrun, s[:, j * LANES:(j + 1) * LANES])
        s_new = lax.dot_general(qg[g], kn[:, g * HEAD_DIM:(g + 1) * HEAD_DIM], _NT,
                                preferred_element_type=F32) + tile_bias(new_lo, LANES)
        s_ref[rows, new_lo:] = s_new
        ms.append(jnp.max(jnp.maximum(mrun, s_new), axis=-1, keepdims=True))

    _wait_pages(cv_hbm, vbuf, slot, vsem)
    for g in range(N_KV_HEADS):
        rows = slice(g * half, (g + 1) * half)
        lrun = jnp.zeros((half, LANES), F32)
        acc = jnp.zeros((half, HEAD_DIM), F32)
        for c in range(nchunk):
            p = jnp.exp2(s_ref[rows, c * cw:(c + 1) * cw] - ms[g])
            for j in range(cw // LANES):
                lrun = lrun + p[:, j * LANES:(j + 1) * LANES]
            acc = acc + lax.dot_general(p.astype(BF16), chunk_t(vbuf, c, g), _NT,
                                        preferred_element_type=F32)
        p = jnp.exp2(s_ref[rows, new_lo:] - ms[g])
        lrun = lrun + p
        acc = acc + jnp.dot(p.astype(BF16), vn[:, g * HEAD_DIM:(g + 1) * HEAD_DIM],
                            preferred_element_type=F32)
        o = acc / jnp.sum(lrun, axis=-1, keepdims=True)
        for hh in range(GROUP):
            h = GROUP * g + hh
            a_ref[:, h * HEAD_DIM:(h + 1) * HEAD_DIM] = o[hh * td:(hh + 1) * td, :].astype(BF16)


def _sample_attention(page_table, q, bias, k_new, v_new, cache_kt, cache_vt, td, chunk_pages=4):
    bd, npages = page_table.shape
    page = cache_kt.shape[3]
    nk = bias.shape[1]
    rowb = lambda b, pt: (b, 0)
    grid_spec = pltpu.PrefetchScalarGridSpec(
        num_scalar_prefetch=1, grid=(bd,),
        in_specs=[
            pl.BlockSpec((td, ATTN_WIDTH), rowb),
            pl.BlockSpec((td, nk), rowb),
            pl.BlockSpec((td, KV_WIDTH), rowb),
            pl.BlockSpec((td, KV_WIDTH), rowb),
            pl.BlockSpec(memory_space=pl.ANY),
            pl.BlockSpec(memory_space=pl.ANY),
        ],
        out_specs=pl.BlockSpec((td, ATTN_WIDTH), rowb),
        scratch_shapes=[
            pltpu.VMEM((2, npages, N_KV_HEADS, HEAD_DIM, page), F32),
            pltpu.VMEM((2, npages, N_KV_HEADS, HEAD_DIM, page), F32),
            pltpu.VMEM((N_HEADS * td, nk), F32),
            pltpu.SemaphoreType.DMA((2,)),
            pltpu.SemaphoreType.DMA((2,)),
        ],
    )
    return pl.pallas_call(
        functools.partial(_sample_attn_kernel, chunk_pages=chunk_pages),
        grid_spec=grid_spec, out_shape=jax.ShapeDtypeStruct((bd * td, ATTN_WIDTH), BF16),
        compiler_params=_cparams(("arbitrary",)), name="sample_dsa",
    )(page_table, q, bias, k_new, v_new, cache_kt, cache_vt)


def _layer_weights(i, g_mix, w_in, g_idx_k, b_idx_k, w_dw, b_dw, g_conv_ln, b_conv_ln, w_out, g_ffn, w_up,
                   w_ffn_conv, b_ffn_conv, w_down, g_ple, w_ple_gate, w_ple, g_final):
    d = w_in.shape[1]
    w = w_in[i]
    n_qkvi = C_KI + IDX_DIM + IDX_HEADS
    pad = jnp.zeros((d, LANES - IDX_DIM - IDX_HEADS), w.dtype)
    w_comb = jnp.concatenate([w[:, :n_qkvi], pad, w[:, n_qkvi:]], axis=1).astype(BF16)
    zpad = jnp.zeros((LANES - IDX_DIM,), F32)
    gik = jnp.concatenate([g_idx_k[i], zpad])[None]
    bik = jnp.concatenate([b_idx_k[i], zpad])[None]
    proj = (g_mix[i][None], w_comb, gik, bik)

    wo = w_out[i].astype(BF16)
    conv = (w_dw[i], b_dw[i][None], g_conv_ln[i][None], b_conv_ln[i][None], wo[:ATTN_WIDTH], wo[ATTN_WIDTH:])

    d_ff = w_down.shape[1]
    nch = d_ff // FFN_CW
    ffn = (g_ffn[i][None], _ffn_chunks(w_up[i].astype(BF16), nch), _ffn_chunks(w_ffn_conv[i], nch),
           _ffn_chunks(b_ffn_conv[i][None], nch), w_down[i].astype(BF16).reshape(nch, FFN_CW, d),
           g_ple[i][None], w_ple_gate[i].astype(BF16), w_ple[i].astype(BF16), g_final[None])
    return proj, conv, ffn


def _ffn_chunks(m, nch):
    rows = m.shape[0]
    return m.reshape(rows, 2, nch, FFN_CW).transpose(2, 0, 1, 3).reshape(nch, rows, 2 * FFN_CW)


def _ffn_unchunk(st):
    nch, rows, cw2 = st.shape
    return st.reshape(nch, rows, 2, cw2 // 2).transpose(1, 2, 0, 3).reshape(rows, nch * cw2)


def kernel(x_prompt, x_sample, p_prompt, p_sample, cache_k, cache_v, cache_idx_k, state_conv, state_ffn_conv,
           page_table, g_mix, w_in, g_idx_k, b_idx_k, w_dw, b_dw, g_conv_ln, b_conv_ln, w_out, g_ffn, w_up,
           w_ffn_conv, b_ffn_conv, w_down, g_ple, w_ple_gate, w_ple, g_final):
    B, S, D = x_prompt.shape
    Bd, Td, _ = x_sample.shape
    depth = w_in.shape[0]
    n_pages = page_table.shape[1]
    page = cache_k.shape[2]
    past_len = n_pages * page
    d_ff = w_down.shape[1]
    nch = d_ff // FFN_CW
    topk_prompt = min(TOPK_MAX, S // 4)
    topk_sample = min(TOPK_MAX, (past_len + Td) // 4)
    assert S % KC == 0 and Td == SUBLANES and d_ff % FFN_CW == 0 and page == LANES

    tm_p = KC
    BL = min(32, Bd)
    G = Bd // BL
    tm_s = Td * BL

    tabs_p = _rope_tables(jnp.arange(S, dtype=I32))
    pos_s = past_len + jnp.arange(Td, dtype=I32)
    tabs_s = tuple(jnp.tile(t, (Bd, 1)) for t in _rope_tables(pos_s))

    def to_tm(a):
        return a.reshape(G, BL, Td, a.shape[-1]).transpose(0, 2, 1, 3).reshape(G * tm_s, a.shape[-1])

    def from_tm(a):
        return a.reshape(G, Td, BL, a.shape[-1]).transpose(0, 2, 1, 3).reshape(Bd, Td, a.shape[-1])

    hp = x_prompt.reshape(B * S, D)
    hs = x_sample.reshape(Bd * Td, D)
    outs = {k: [] for k in ("kp", "vp", "ip", "cp", "fp", "ks", "vs", "is", "cs", "fs")}
    for i in range(depth):
        last = i == depth - 1
        proj_w, conv_w, ffn_w = _layer_weights(
            i, g_mix, w_in, g_idx_k, b_idx_k, w_dw, b_dw, g_conv_ln, b_conv_ln, w_out, g_ffn, w_up,
            w_ffn_conv, b_ffn_conv, w_down, g_ple, w_ple_gate, w_ple, g_final)

        q, qit, kt, vt, kit, ktb, vb, kib, wit, glu = _project(
            hp, tabs_p, S // tm_p, *proj_w, tm=tm_p, prompt=True, batch=B)
        a = _prompt_attention(q, qit, wit, kib, ktb, vb, B, S, topk_prompt)
        nt = S // tm_p
        halo_rows = 32
        halo_spec = pl.BlockSpec(
            (halo_rows, CONV_CH), lambda b, t: (jnp.maximum((b * nt + t) * (tm_p // halo_rows) - 1, 0), 0))
        h1 = _conv_out(glu, glu, halo_spec, a, hp, *conv_w, groups=B, nt=nt, tm=tm_p, shift=1, zero_first=True)
        st0 = jnp.zeros((B, nch, SUBLANES, 2 * FFN_CW), F32)
        hp, st = _ffn(h1, p_prompt[i].reshape(B * S, -1), st0, ffn_w, groups=B, nt=nt, tm=tm_p,
                      shift=1, halo=SUBLANES, final=last)
        outs["kp"].append(kt.transpose(0, 3, 1, 2))
        outs["vp"].append(vt.transpose(0, 3, 1, 2))
        outs["ip"].append(kit.transpose(0, 2, 1))
        outs["cp"].append(glu.reshape(B, S, CONV_CH)[:, S - (CONV_W - 1):])
        outs["fp"].append(jax.vmap(_ffn_unchunk)(st)[:, SUBLANES - (FFN_CONV_W - 1):])

        tm_sp = min(512, Bd * Td)
        q, qi, k, v, ki, kib, wi, glu = _project(hs, tabs_s, Bd * Td // tm_sp, *proj_w, tm=tm_sp, prompt=False)
        keys = _sample_scores(page_table, qi, wi, kib, cache_idx_k[i].transpose(0, 2, 1), Td)
        bias = _select(keys, topk_sample, rows=min(64, Bd * Td))
        a = _sample_attention(page_table, q, bias, k, v,
                              cache_k[i].transpose(0, 2, 3, 1), cache_v[i].transpose(0, 2, 3, 1), Td)
        sc = state_conv[i]
        halo_s = sc.reshape(G, BL, CONV_W - 1, CONV_CH).transpose(0, 2, 1, 3).reshape(G * (CONV_W - 1) * BL, CONV_CH)
        halo_spec = pl.BlockSpec(((CONV_W - 1) * BL, CONV_CH), lambda g, t: (g, 0))
        h1 = _conv_out(to_tm(glu), halo_s, halo_spec, to_tm(a), to_tm(hs), *conv_w, groups=G, nt=1, tm=tm_s,
                       shift=BL, zero_first=False)
        sf = state_ffn_conv[i]
        sf = sf.reshape(G, BL, FFN_CONV_W - 1, 2 * d_ff).transpose(0, 2, 1, 3).reshape(G, (FFN_CONV_W - 1) * BL, 2 * d_ff)
        st0 = jax.vmap(lambda s: _ffn_chunks(s, nch))(sf)
        hs_tm, st = _ffn(h1, to_tm(p_sample[i].reshape(Bd * Td, -1)), st0, ffn_w, groups=G, nt=1, tm=tm_s,
                         shift=BL, halo=(FFN_CONV_W - 1) * BL, final=last)
        hs = from_tm(hs_tm).reshape(Bd * Td, D)
        fs = jax.vmap(_ffn_unchunk)(st)
        fs = fs.reshape(G, FFN_CONV_W - 1, BL, 2 * d_ff).transpose(0, 2, 1, 3).reshape(Bd, FFN_CONV_W - 1, 2 * d_ff)
        outs["ks"].append(k.reshape(Bd, Td, N_KV_HEADS, HEAD_DIM))
        outs["vs"].append(v.reshape(Bd, Td, N_KV_HEADS, HEAD_DIM))
        outs["is"].append(ki.reshape(Bd, Td, IDX_DIM))
        outs["cs"].append(jnp.concatenate([sc, glu.reshape(Bd, Td, CONV_CH)], axis=1)[:, Td:])
        outs["fs"].append(fs)

    y_prompt = hp.reshape(B, S, D)
    y_sample = hs.reshape(Bd, Td, D)
    st = lambda name: jnp.stack(outs[name])
    return (y_prompt, y_sample, st("kp"), st("vp"), st("ip"), st("cp"), st("fp"),
            st("ks"), st("vs"), st("is"), st("cs"), st("fs"))
```

```python
import functools

import jax
import jax.numpy as jnp
from jax import lax
from jax.experimental import pallas as pl
from jax.experimental.pallas import tpu as pltpu

N_HEADS = 8
HEAD_DIM = 64
N_KV_HEADS = 2
GROUP = N_HEADS // N_KV_HEADS
ATTN_WIDTH = N_HEADS * HEAD_DIM
KV_WIDTH = N_KV_HEADS * HEAD_DIM
IDX_HEADS = 8
IDX_DIM = 64
IDX_ROPE_DIM = 32
TOPK_MAX = 256
CONV_W = 31
FFN_CONV_W = 3
ROPE_THETA = 10000.0
EPS = 1e-6
NEG = -1e30

LANES = 128
SUBLANES = 8
INT_MIN = -2 ** 31
VMEM_LIMIT = 56 * 1024 * 1024

TQ = LANES
TPC = 4
KC = TPC * LANES
CPC = TPC
Q_SCALE = HEAD_DIM ** -0.5 * 1.4426950408889634
BISECT_FLOAT_PASSES = 16
BISECT_MIN_PASSES = 12
BISECT_CHECK_EVERY = 4
BISECT_MAX_PASSES = 48

F32 = jnp.float32
BF16 = jnp.bfloat16
I32 = jnp.int32

_NT = (((1,), (1,)), ((), ()))


def _cparams(sem):
    return pltpu.CompilerParams(dimension_semantics=sem, vmem_limit_bytes=VMEM_LIMIT)


def _sortable(x):
    b = pltpu.bitcast(x, I32)
    return b ^ ((b >> 31) & jnp.int32(0x7FFFFFFF))


def _rmsnorm(x, g):
    return x * lax.rsqrt(jnp.mean(x * x, axis=-1, keepdims=True) + EPS) * g


C_Q = 0
C_K = C_Q + ATTN_WIDTH
C_V = C_K + KV_WIDTH
C_QI = C_V + KV_WIDTH
C_KI = C_QI + IDX_HEADS * IDX_DIM
C_A = C_KI + LANES
CONV_CH = 512
C_G = C_A + CONV_CH
C_END = C_G + CONV_CH


def _proj_kernel(x_ref, g_ref, w_ref, gik_ref, bik_ref, cq_ref, sq_ref, ci_ref, si_ref, *out_refs,
                 wi_scale, prompt):
    tm = x_ref.shape[0]
    hn = _rmsnorm(x_ref[...], g_ref[...]).astype(BF16)
    z = jnp.dot(hn, w_ref[...], preferred_element_type=F32)

    lane = lax.broadcasted_iota(I32, (tm, LANES), 1)
    in_head = lane % HEAD_DIM
    cq, sq, ci, si = cq_ref[...], sq_ref[...], ci_ref[...], si_ref[...]

    def rope_full(xg):
        sw = jnp.where(in_head < HEAD_DIM // 2, pltpu.roll(xg, LANES - HEAD_DIM // 2, 1),
                       pltpu.roll(xg, HEAD_DIM // 2, 1))
        return xg * cq + sw * sq

    def rope_part(xg):
        sw = jnp.where(in_head < IDX_ROPE_DIM // 2, pltpu.roll(xg, LANES - IDX_ROPE_DIM // 2, 1),
                       pltpu.roll(xg, IDX_ROPE_DIM // 2, 1))
        return xg * ci + sw * si

    q_groups = [rope_full(z[:, C_Q + g * LANES:C_Q + (g + 1) * LANES]) * Q_SCALE
                for g in range(ATTN_WIDTH // LANES)]
    qi_groups = [rope_part(z[:, C_QI + g * LANES:C_QI + (g + 1) * LANES])
                 for g in range(IDX_HEADS * IDX_DIM // LANES)]
    kr = rope_full(z[:, C_K:C_K + LANES])
    vr = z[:, C_V:C_V + LANES]

    zg = z[:, C_KI:C_KI + LANES]
    is_ki = lane < IDX_DIM
    mu = jnp.sum(jnp.where(is_ki, zg, 0.0), axis=-1, keepdims=True) / IDX_DIM
    xc = jnp.where(is_ki, zg - mu, 0.0)
    var = jnp.sum(xc * xc, axis=-1, keepdims=True) / IDX_DIM
    kin = rope_part(xc * lax.rsqrt(var + EPS) * gik_ref[...] + bik_ref[...])
    wig = zg * wi_scale
    glu = z[:, C_A:C_A + CONV_CH] * jax.nn.sigmoid(z[:, C_G:C_G + CONV_CH])

    if prompt:
        q_ref, qit_ref, kt_ref, vt_ref, kit_ref, ktb_ref, vb_ref, kib_ref, wit_ref, glu_ref = out_refs
        for g, qg in enumerate(q_groups):
            q_ref[:, g * LANES:(g + 1) * LANES] = qg.astype(BF16)
        for g, qig in enumerate(qi_groups):
            qit_ref[g * LANES:(g + 1) * LANES, :] = qig.T.astype(BF16)
        krt = kr.T
        vrt = vr.T
        for g in range(N_KV_HEADS):
            rows = slice(g * HEAD_DIM, (g + 1) * HEAD_DIM)
            kt_ref[0, g] = krt[rows]
            vt_ref[0, g] = vrt[rows]
            ktb_ref[0, g] = krt[rows].astype(BF16)
            vg = vr if g == 0 else pltpu.roll(vr, LANES - g * HEAD_DIM, 1)
            vb_ref[g] = jnp.where(lane < HEAD_DIM, vg, jnp.where(lane == HEAD_DIM, 1.0, 0.0)).astype(BF16)
        kit_ref[0] = kin.T[:IDX_DIM]
        kib_ref[...] = kin[:, :IDX_DIM].astype(BF16)
        wit_ref[...] = wig.T[IDX_DIM:IDX_DIM + IDX_HEADS, :]
        glu_ref[...] = glu
    else:
        q_ref, qi_ref, k_ref, v_ref, ki_ref, kib_ref, wi_ref, glu_ref = out_refs
        for g, qg in enumerate(q_groups):
            q_ref[:, g * LANES:(g + 1) * LANES] = qg.astype(BF16)
        for g, qig in enumerate(qi_groups):
            qi_ref[:, g * LANES:(g + 1) * LANES] = qig.astype(BF16)
        k_ref[...] = kr
        v_ref[...] = vr
        ki_ref[...] = kin[:, :IDX_DIM]
        kib_ref[...] = kin[:, :IDX_DIM].astype(BF16)
        wi_ref[...] = wig[:, IDX_DIM:IDX_DIM + IDX_HEADS]
        glu_ref[...] = glu


def _rope_tables(pos):
    def cs(half):
        inv = jnp.power(jnp.float32(ROPE_THETA), -jnp.arange(half, dtype=F32) / half)
        ang = pos.astype(F32)[:, None] * inv[None, :]
        return jnp.cos(ang), jnp.sin(ang)

    c, s = cs(HEAD_DIM // 2)
    cq = jnp.concatenate([c, c, c, c], axis=-1)
    sq = jnp.concatenate([-s, s, -s, s], axis=-1)
    c2, s2 = cs(IDX_ROPE_DIM // 2)
    one = jnp.ones((pos.shape[0], IDX_DIM - IDX_ROPE_DIM), F32)
    ci = jnp.concatenate([c2, c2, one, c2, c2, one], axis=-1)
    si = jnp.concatenate([-s2, s2, 0 * one, -s2, s2, 0 * one], axis=-1)
    return cq, sq, ci, si


def _project(x, tables, tab_blocks, g_mix, w_comb, gik, bik, *, tm, prompt, batch=None):
    n, d = x.shape
    nb = n // tm
    row = lambda i: (i, 0)
    const = lambda i: (0, 0)
    tab = lambda i: (i % tab_blocks, 0)
    wi_scale = IDX_HEADS ** -0.5 * IDX_DIM ** -0.5
    sds = jax.ShapeDtypeStruct
    if prompt:
        seq = n // batch
        nt = seq // tm
        bt = lambda i: (i // nt, 0, 0, i % nt)
        out_shape = (
            sds((n, ATTN_WIDTH), BF16),
            sds((IDX_HEADS * IDX_DIM, n), BF16),
            sds((batch, N_KV_HEADS, HEAD_DIM, seq), F32),
            sds((batch, N_KV_HEADS, HEAD_DIM, seq), F32),
            sds((batch, IDX_DIM, seq), F32),
            sds((nb, N_KV_HEADS, HEAD_DIM, tm), BF16),
            sds((N_KV_HEADS, n, LANES), BF16),
            sds((n, IDX_DIM), BF16),
            sds((IDX_HEADS, n), F32),
            sds((n, CONV_CH), F32),
        )
        out_specs = (
            pl.BlockSpec((tm, ATTN_WIDTH), row),
            pl.BlockSpec((IDX_HEADS * IDX_DIM, tm), lambda i: (0, i)),
            pl.BlockSpec((1, N_KV_HEADS, HEAD_DIM, tm), bt),
            pl.BlockSpec((1, N_KV_HEADS, HEAD_DIM, tm), bt),
            pl.BlockSpec((1, IDX_DIM, tm), lambda i: (i // nt, 0, i % nt)),
            pl.BlockSpec((1, N_KV_HEADS, HEAD_DIM, tm), lambda i: (i, 0, 0, 0)),
            pl.BlockSpec((N_KV_HEADS, tm, LANES), lambda i: (0, i, 0)),
            pl.BlockSpec((tm, IDX_DIM), row),
            pl.BlockSpec((IDX_HEADS, tm), lambda i: (0, i)),
            pl.BlockSpec((tm, CONV_CH), row),
        )
    else:
        out_shape = (
            sds((n, ATTN_WIDTH), BF16),
            sds((n, IDX_HEADS * IDX_DIM), BF16),
            sds((n, KV_WIDTH), F32),
            sds((n, KV_WIDTH), F32),
            sds((n, IDX_DIM), F32),
            sds((n, IDX_DIM), BF16),
            sds((n, IDX_HEADS), F32),
            sds((n, CONV_CH), F32),
        )
        out_specs = (
            pl.BlockSpec((tm, ATTN_WIDTH), row),
            pl.BlockSpec((tm, IDX_HEADS * IDX_DIM), row),
            pl.BlockSpec((tm, KV_WIDTH), row),
            pl.BlockSpec((tm, KV_WIDTH), row),
            pl.BlockSpec((tm, IDX_DIM), row),
            pl.BlockSpec((tm, IDX_DIM), row),
            pl.BlockSpec((tm, IDX_HEADS), row),
            pl.BlockSpec((tm, CONV_CH), row),
        )
    in_specs = [
        pl.BlockSpec((tm, d), row),
        pl.BlockSpec((1, d), const),
        pl.BlockSpec((d, C_END), const),
        pl.BlockSpec((1, LANES), const),
        pl.BlockSpec((1, LANES), const),
    ] + [pl.BlockSpec((tm, LANES), tab)] * 4
    return pl.pallas_call(
        functools.partial(_proj_kernel, wi_scale=wi_scale, prompt=prompt),
        grid=(nb,), in_specs=in_specs, out_specs=out_specs, out_shape=out_shape,
        compiler_params=_cparams(("parallel",)), name="in_proj",
    )(x, g_mix, w_comb, gik, bik, *tables)


def _tile_count(pred):
    return jnp.sum(pred.astype(I32).reshape(LANES // SUBLANES, SUBLANES, LANES), axis=0)


def _pair_loop(n, body, init):
    carry = lax.fori_loop(0, n // 2, lambda k, cr: body(2 * k + 1, body(2 * k, cr)), init)
    return lax.cond(n % 2 == 1, lambda cr: body(n - 1, cr), lambda cr: cr, carry)


def _count_keys(key_ref, nsteps, pred):
    def body(c, accs):
        return tuple(a + _tile_count(pred(key_ref[c * CPC + j], c * CPC + j)) for j, a in enumerate(accs))
    z8 = jnp.zeros((SUBLANES, LANES), I32)
    accs = lax.fori_loop(0, nsteps, body, (z8,) * CPC)
    return jnp.sum(sum(accs[1:], accs[0]), axis=0, keepdims=True)


def _prompt_attn_kernel(qit_ref, wit_ref, q_ref, kib_ref, ktb_ref, vb_ref, a_ref,
                        key_ref, bias_ref, s_ref, m_ref, acc_ref, *, topk):
    i = pl.program_id(1)
    nch = i // TPC + 1
    r_io = lax.broadcasted_iota(I32, (TQ, TQ), 0)
    c_io = lax.broadcasted_iota(I32, (TQ, TQ), 1)

    def causal(kt):
        return r_io + (kt - i) * TQ <= c_io

    qit = jnp.concatenate([qit_ref[h * IDX_DIM:(h + 1) * IDX_DIM, :] for h in range(IDX_HEADS)], axis=1)
    w_rows = [wit_ref[h:h + 1, :] for h in range(IDX_HEADS)]

    def score_body(c, mm):
        kmin, kmax = mm
        for j in range(TPC):
            kt = c * TPC + j
            kit = kib_ref[pl.ds(pl.multiple_of(kt * TQ, TQ), TQ), :]
            lg = jnp.dot(kit, qit, preferred_element_type=F32)
            sc = jnp.maximum(lg[:, :TQ], 0.0) * w_rows[0]
            for h in range(1, IDX_HEADS):
                sc = sc + jnp.maximum(lg[:, h * TQ:(h + 1) * TQ], 0.0) * w_rows[h]
            vis = causal(kt)
            k = _sortable(sc)
            key_ref[kt] = jnp.where(vis, k, INT_MIN)
            kmax = jnp.maximum(kmax, jnp.where(vis, k, INT_MIN))
            kmin = jnp.minimum(kmin, jnp.where(vis, k, jnp.int32(2 ** 31 - 1)))
        return kmin, kmax
    kmin, kmax = _pair_loop(nch, score_body, (jnp.full((TQ, TQ), 2 ** 31 - 1, I32),
                                              jnp.full((TQ, TQ), INT_MIN, I32)))

    n_keys = i * TQ + c_io[0:1, :] + 1
    enough = n_keys >= topk
    lo0 = jnp.min(kmin, axis=0, keepdims=True)
    hi0 = jnp.max(kmax, axis=0, keepdims=True) + 1

    def unsort(k):
        return pltpu.bitcast(k ^ ((k >> 31) & jnp.int32(0x7FFFFFFF)), F32)

    def bisect_pass(p, lo, hi, cnt):
        mid_i = (lo >> 1) + (hi >> 1) + (lo & hi & 1)
        mid_f = _sortable(0.5 * unsort(lo) + 0.5 * unsort(hi))
        use_f = (p < BISECT_FLOAT_PASSES) & (mid_f > lo) & (mid_f < hi)
        mid = jnp.where(use_f, mid_f, mid_i)
        c = _count_keys(key_ref, nch, lambda k, kt: k >= mid)
        ge = c >= topk
        return jnp.where(ge, mid, lo), jnp.where(ge, hi, mid), jnp.where(ge, c, cnt)

    def bisect_cond(st):
        p, done = st[0], st[1]
        return jnp.logical_and(jnp.logical_not(done), p < BISECT_MAX_PASSES)

    def bisect_body(st):
        p, _, lo, hi, cnt = st
        for u in range(BISECT_CHECK_EVERY):
            lo, hi, cnt = bisect_pass(p + u, lo, hi, cnt)
        settled = (cnt == topk) | (lo + 1 >= hi) | jnp.logical_not(enough)
        return p + BISECT_CHECK_EVERY, jnp.min(settled.astype(I32)) > 0, lo, hi, cnt

    lo1, hi1, cnt1 = lax.fori_loop(0, BISECT_MIN_PASSES, lambda p, st: bisect_pass(p, *st),
                                   (lo0, hi0, n_keys))
    _, _, lo, _, _ = lax.while_loop(bisect_cond, bisect_body,
                                    (jnp.int32(BISECT_MIN_PASSES), jnp.bool_(False), lo1, hi1, cnt1))
    thr = jnp.where(enough, lo, INT_MIN)

    n_gt = _count_keys(key_ref, nch, lambda k, kt: k > thr)
    n_eq = _count_keys(key_ref, nch, lambda k, kt: k == thr)
    need = topk - n_gt
    n_idx_bits = max(1, (key_ref.shape[0] * TQ).bit_length())

    def tie_search(_):
        def body(p, m):
            cand = m | jnp.left_shift(jnp.int32(1), n_idx_bits - 1 - p)
            below = _count_keys(key_ref, nch, lambda k, kt: (k == thr) & (r_io + kt * TQ < cand))
            return jnp.where(below < need, cand, m)
        return lax.fori_loop(0, n_idx_bits, body, jnp.zeros((1, TQ), I32))

    excess = jnp.max(jnp.where((need > 0) & (thr > INT_MIN), n_eq - need, 0)) > 0
    idx_max = lax.cond(excess, tie_search,
                       lambda _: jnp.full((1, TQ), key_ref.shape[0] * TQ, I32), 0)

    def write_bias(c):
        for j in range(TPC):
            kt = c * TPC + j
            k = key_ref[kt]
            sel = ((k > thr) | ((k == thr) & (r_io + kt * TQ <= idx_max))) & causal(kt)
            bias_ref[kt] = jnp.where(sel, 0.0, NEG).astype(F32).T.astype(BF16)
    last = nch - 1
    write_bias(0)
    write_bias(jnp.minimum(1, last))

    q = q_ref[...]
    row_q = lax.broadcasted_iota(I32, (GROUP * TQ, TQ), 0) % TQ
    onehot = (row_q == lax.broadcasted_iota(I32, (GROUP * TQ, TQ), 1)).astype(BF16)
    lhs = [jnp.concatenate([onehot] + [jnp.concatenate(
        [q[:, (GROUP * g + hh) * HEAD_DIM:(GROUP * g + hh + 1) * HEAD_DIM] for hh in range(GROUP)], axis=0)],
        axis=1) for g in range(N_KV_HEADS)]
    m_ref[...] = jnp.full(m_ref.shape, NEG, F32)

    def qk_chunk(c):
        bias = jnp.concatenate([bias_ref[c * TPC + j] for j in range(TPC)], axis=1)
        for g in range(N_KV_HEADS):
            rhs = jnp.concatenate([bias, ktb_ref[c, g]], axis=0)
            s = jnp.dot(lhs[g], rhs, preferred_element_type=F32)
            s_ref[g, c] = s
            cm = s[:, :TQ]
            for j in range(1, TPC):
                cm = jnp.maximum(cm, s[:, j * TQ:(j + 1) * TQ])
            m_ref[g] = jnp.maximum(m_ref[g], cm)

    def qk_pair(k, carry):
        qk_chunk(2 * k)
        qk_chunk(2 * k + 1)
        write_bias(jnp.minimum(2 * k + 2, last))
        write_bias(jnp.minimum(2 * k + 3, last))
        return carry
    lax.fori_loop(0, nch // 2, qk_pair, 0)

    @pl.when(nch % 2 == 1)
    def _():
        qk_chunk(last)
    m = [jnp.max(m_ref[g], axis=-1, keepdims=True) for g in range(N_KV_HEADS)]
    acc_ref[...] = jnp.zeros(acc_ref.shape, F32)

    def pv_chunk(g, c):
        p = jnp.exp2(s_ref[g, c] - m[g]).astype(BF16)
        vc = vb_ref[g, pl.ds(pl.multiple_of(c * KC, KC), KC), :]
        return jnp.dot(p, vc, preferred_element_type=F32)

    def pv_pair(k, carry):
        for g in range(N_KV_HEADS):
            acc_ref[g] += pv_chunk(g, 2 * k) + pv_chunk(g, 2 * k + 1)
        return carry
    lax.fori_loop(0, nch // 2, pv_pair, 0)

    @pl.when(nch % 2 == 1)
    def _():
        for g in range(N_KV_HEADS):
            acc_ref[g] += pv_chunk(g, last)
    for g in range(N_KV_HEADS):
        acc = acc_ref[g]
        o = acc[:, :HEAD_DIM] / acc[:, HEAD_DIM:HEAD_DIM + 1]
        for hh in range(GROUP):
            h = GROUP * g + hh
            a_ref[:, h * HEAD_DIM:(h + 1) * HEAD_DIM] = o[hh * TQ:(hh + 1) * TQ, :].astype(BF16)


def _prompt_attention(q, qit, wit, kib, ktb, vb, batch, seq, topk):
    nb = seq // TQ
    ncs = seq // KC
    qrow = lambda b, i: (b * nb + i, 0)
    qcol = lambda b, i: (0, b * nb + i)
    return pl.pallas_call(
        functools.partial(_prompt_attn_kernel, topk=topk),
        grid=(batch, nb),
        in_specs=[
            pl.BlockSpec((IDX_HEADS * IDX_DIM, TQ), qcol),
            pl.BlockSpec((IDX_HEADS, TQ), qcol),
            pl.BlockSpec((TQ, ATTN_WIDTH), qrow),
            pl.BlockSpec((seq, IDX_DIM), lambda b, i: (b, 0)),
            pl.BlockSpec((ncs, N_KV_HEADS, HEAD_DIM, KC), lambda b, i: (b, 0, 0, 0)),
            pl.BlockSpec((N_KV_HEADS, seq, LANES), lambda b, i: (0, b, 0)),
        ],
        out_specs=pl.BlockSpec((TQ, ATTN_WIDTH), qrow),
        out_shape=jax.ShapeDtypeStruct((batch * seq, ATTN_WIDTH), BF16),
        scratch_shapes=[
            pltpu.VMEM((nb, TQ, TQ), I32),
            pltpu.VMEM((nb, TQ, TQ), BF16),
            pltpu.VMEM((N_KV_HEADS, ncs, GROUP * TQ, KC), F32),
            pltpu.VMEM((N_KV_HEADS, GROUP * TQ, TQ), F32),
            pltpu.VMEM((N_KV_HEADS, GROUP * TQ, LANES), F32),
        ],
        compiler_params=_cparams(("parallel", "arbitrary")), name="prompt_dsa",
    )(qit, wit, q, kib, ktb, vb)


def _conv_out_kernel(glu_ref, halo_ref, a_ref, x_ref, wdw_ref, bdw_ref, gln_ref, bln_ref, wa_ref, wc_ref,
                     h_ref, xp_ref, c_ref, *maybe_xs_ref, shift, zero_first, rc):
    tm = glu_ref.shape[0]
    halo = halo_ref.shape[0]
    hv = halo_ref[...]
    if zero_first:
        hv = jnp.where(pl.program_id(1) == 0, 0.0, hv)
    xp_ref[0:halo, :] = hv
    xp_ref[halo:halo + tm, :] = glu_ref[...]
    if shift % SUBLANES:
        (xs_ref,) = maybe_xs_ref
        for r in range(1, SUBLANES):
            xs_ref[r - 1] = xp_ref[r:r + xs_ref.shape[1], :]

    def window(off, ls):
        r = off % SUBLANES
        if r == 0:
            return xp_ref[off:off + rc, ls]
        return xs_ref[r - 1, off - r:off - r + rc, ls]

    for lg in range(CONV_CH // LANES):
        ls = slice(lg * LANES, (lg + 1) * LANES)
        for r0 in range(0, tm, rc):
            acc = jnp.zeros((rc, LANES), F32)
            for j in range(CONV_W):
                off = halo - (CONV_W - 1 - j) * shift
                acc = acc + wdw_ref[j:j + 1, ls] * window(r0 + off, ls)
            c_ref[r0:r0 + rc, ls] = acc + bdw_ref[:, ls]

    c = c_ref[...]
    mu = jnp.mean(c, axis=-1, keepdims=True)
    xc = c - mu
    var = jnp.mean(xc * xc, axis=-1, keepdims=True)
    y = xc * lax.rsqrt(var + EPS) * gln_ref[...] + bln_ref[...]
    y = jax.nn.silu(y)
    h_ref[...] = (x_ref[...]
                  + jnp.dot(a_ref[...], wa_ref[...], preferred_element_type=F32)
                  + jnp.dot(y.astype(BF16), wc_ref[...], preferred_element_type=F32))


def _conv_out(glu, halo_arr, halo_spec, a, x, w_dw, b_dw, g_ln, b_ln, w_a, w_c, *, groups, nt, tm,
              shift, zero_first):
    d = x.shape[1]
    halo = halo_spec.block_shape[0]
    row = lambda b, i: (b * nt + i, 0)
    const = lambda b, i: (0, 0)
    return pl.pallas_call(
        functools.partial(_conv_out_kernel, shift=shift, zero_first=zero_first, rc=32),
        grid=(groups, nt),
        in_specs=[
            pl.BlockSpec((tm, CONV_CH), row),
            halo_spec,
            pl.BlockSpec((tm, ATTN_WIDTH), row),
            pl.BlockSpec((tm, d), row),
            pl.BlockSpec((CONV_W, CONV_CH), const),
            pl.BlockSpec((1, CONV_CH), const),
            pl.BlockSpec((1, CONV_CH), const),
            pl.BlockSpec((1, CONV_CH), const),
            pl.BlockSpec((ATTN_WIDTH, d), const),
            pl.BlockSpec((CONV_CH, d), const),
        ],
        out_specs=pl.BlockSpec((tm, d), row),
        out_shape=jax.ShapeDtypeStruct(x.shape, F32),
        scratch_shapes=[pltpu.VMEM((halo + tm, CONV_CH), F32), pltpu.VMEM((tm, CONV_CH), F32)]
        + ([pltpu.VMEM((SUBLANES - 1, halo + tm - SUBLANES, CONV_CH), F32)] if shift % SUBLANES else []),
        compiler_params=_cparams(("parallel", "arbitrary")), name="conv_out_proj",
    )(glu, halo_arr, a, x, w_dw, b_dw, g_ln, b_ln, w_a, w_c)


FFN_CW = 256
FFN_RC = 32


def _ffn_kernel(h_ref, p_ref, st_ref, gffn_ref, wu_ref, wf_ref, bf_ref, wdn_ref, gple_ref, wgate_ref, wple_ref,
                gfin_ref, y_ref, sto_ref, u_ref, act_ref, carry_ref, *, shift, halo, final):
    tm = h_ref.shape[0]
    nch = wu_ref.shape[0]
    cw = wu_ref.shape[2] // 2
    first = pl.program_id(1) == 0
    h = h_ref[...]
    hn = _rmsnorm(h, gffn_ref[...]).astype(BF16)

    @pl.when(first)
    def _():
        carry_ref[...] = st_ref[...]

    def up(c, buf):
        ub = u_ref.at[buf]
        ub[0:halo, :] = carry_ref[c]
        ub[halo:halo + tm, :] = jnp.dot(hn, wu_ref[c], preferred_element_type=F32)
        tail = ub[tm:tm + halo, :]
        carry_ref[c] = tail
        sto_ref[c] = tail

    def act(c, buf):
        wf = wf_ref[c]
        bf = bf_ref[c]
        for r0 in range(0, tm, FFN_RC):
            out = bf
            for k in range(FFN_CONV_W):
                off = r0 + halo - (FFN_CONV_W - 1 - k) * shift
                out = out + wf[k:k + 1, :] * u_ref[buf, off:off + FFN_RC, :]
            act_ref[c, r0:r0 + FFN_RC, :] = (jax.nn.silu(out[:, :cw]) * out[:, cw:]).astype(BF16)

    up(0, 0)
    n_pairs = (nch - 1) // 2

    def pair(k, carry):
        up(2 * k + 1, 1)
        act(2 * k, 0)
        up(2 * k + 2, 0)
        act(2 * k + 1, 1)
        return carry
    lax.fori_loop(0, n_pairs, pair, 0)
    if (nch - 1) % 2:
        up(nch - 1, 1)
        act(nch - 2, 0)
        act(nch - 1, 1)
    else:
        act(nch - 1, 0)

    down = jnp.dot(act_ref[0], wdn_ref[0], preferred_element_type=F32)
    for c in range(1, nch):
        down = down + jnp.dot(act_ref[c], wdn_ref[c], preferred_element_type=F32)
    h2 = h + down
    gate = jax.nn.sigmoid(jnp.dot(_rmsnorm(h2, gple_ref[...]).astype(BF16), wgate_ref[...],
                                  preferred_element_type=F32))
    ple = jnp.dot(p_ref[...].astype(BF16), wple_ref[...], preferred_element_type=F32)
    h3 = h2 + ple * gate
    y_ref[...] = _rmsnorm(h3, gfin_ref[...]) if final else h3


def _ffn(h, p, st, wts, *, groups, nt, tm, shift, halo, final):
    (g_ffn, wu, wf, bf, wdn, g_ple, w_gate, w_ple, g_fin) = wts
    d = h.shape[1]
    nch, _, cw2 = wu.shape
    row = lambda b, i: (b * nt + i, 0)
    c2 = lambda b, i: (0, 0)
    c3 = lambda b, i: (0, 0, 0)
    stspec = pl.BlockSpec((None, nch, halo, cw2), lambda b, i: (b, 0, 0, 0))
    once = dict(pipeline_mode=pl.Buffered(1))
    return pl.pallas_call(
        functools.partial(_ffn_kernel, shift=shift, halo=halo, final=final),
        grid=(groups, nt),
        in_specs=[
            pl.BlockSpec((tm, d), row),
            pl.BlockSpec((tm, p.shape[1]), row),
            stspec,
            pl.BlockSpec((1, d), c2),
            pl.BlockSpec(wu.shape, c3, **once),
            pl.BlockSpec(wf.shape, c3),
            pl.BlockSpec(bf.shape, c3),
            pl.BlockSpec(wdn.shape, c3, **once),
            pl.BlockSpec((1, d), c2),
            pl.BlockSpec(w_gate.shape, c2, **once),
            pl.BlockSpec(w_ple.shape, c2, **once),
            pl.BlockSpec((1, d), c2),
        ],
        out_specs=(pl.BlockSpec((tm, d), row), stspec),
        out_shape=(jax.ShapeDtypeStruct(h.shape, F32),
                   jax.ShapeDtypeStruct((groups, nch, halo, cw2), F32)),
        scratch_shapes=[
            pltpu.VMEM((2, halo + tm, cw2), F32),
            pltpu.VMEM((nch, tm, cw2 // 2), BF16),
            pltpu.VMEM((nch, halo, cw2), F32),
        ],
        compiler_params=_cparams(("parallel", "arbitrary")), name="conv_ffn_ple",
    )(h, p, st, g_ffn, wu, wf, bf, wdn, g_ple, w_gate, w_ple, g_fin)


def _start_pages(pt_ref, b, src_hbm, dst_ref, slot, sem):
    for p in range(dst_ref.shape[1]):
        pltpu.make_async_copy(src_hbm.at[pt_ref[b, p]], dst_ref.at[slot, p], sem.at[slot]).start()


def _wait_pages(src_hbm, dst_ref, slot, sem):
    for p in range(dst_ref.shape[1]):
        pltpu.make_async_copy(src_hbm.at[0], dst_ref.at[slot, p], sem.at[slot]).wait()


def _rows_by_head(x, width):
    return jnp.concatenate([x[:, h * width:(h + 1) * width] for h in range(x.shape[1] // width)], axis=0)


def _pad_rows(x, rows):
    return jnp.concatenate([x, jnp.zeros((rows - x.shape[0], x.shape[1]), x.dtype)], axis=0)


def _sample_score_kernel(pt_ref, qi_ref, wi_ref, kin_ref, cidx_hbm, key_ref, ibuf, sem, *, chunk_pages):
    b = pl.program_id(0)
    nb = pl.num_programs(0)
    slot = b % 2
    npages, page = ibuf.shape[1], ibuf.shape[3]
    td = qi_ref.shape[0]

    @pl.when(b == 0)
    def _():
        _start_pages(pt_ref, b, cidx_hbm, ibuf, 0, sem)

    @pl.when(b + 1 < nb)
    def _():
        _start_pages(pt_ref, b + 1, cidx_hbm, ibuf, 1 - slot, sem)

    qi = _rows_by_head(qi_ref[...], IDX_DIM)
    wcol = jnp.concatenate([wi_ref[:, h:h + 1] for h in range(IDX_HEADS)], axis=0)

    def head_sum(lg):
        r = jnp.maximum(lg, 0.0) * wcol
        return jnp.sum(r.reshape(IDX_HEADS, td, r.shape[1]), axis=0)

    _wait_pages(cidx_hbm, ibuf, slot, sem)
    cw = chunk_pages * page
    for c in range(npages // chunk_pages):
        kt = jnp.concatenate([ibuf[slot, c * chunk_pages + j] for j in range(chunk_pages)], axis=1)
        lg = jnp.dot(qi, kt.astype(BF16), preferred_element_type=F32)
        key_ref[:, c * cw:(c + 1) * cw] = _sortable(head_sum(lg))

    lg = lax.dot_general(qi, _pad_rows(kin_ref[...], LANES), _NT, preferred_element_type=F32)
    sc = head_sum(lg)
    t_io = lax.broadcasted_iota(I32, sc.shape, 0)
    j_io = lax.broadcasted_iota(I32, sc.shape, 1)
    key_ref[:, npages * page:] = jnp.where(j_io <= t_io, _sortable(sc), INT_MIN)


def _sample_scores(page_table, qi, wi, kib, cache_idx_t, td, chunk_pages=4):
    bd, npages = page_table.shape
    page = cache_idx_t.shape[2]
    nk = npages * page + LANES
    grid_spec = pltpu.PrefetchScalarGridSpec(
        num_scalar_prefetch=1, grid=(bd,),
        in_specs=[
            pl.BlockSpec((td, IDX_HEADS * IDX_DIM), lambda b, pt: (b, 0)),
            pl.BlockSpec((td, IDX_HEADS), lambda b, pt: (b, 0)),
            pl.BlockSpec((td, IDX_DIM), lambda b, pt: (b, 0)),
            pl.BlockSpec(memory_space=pl.ANY),
        ],
        out_specs=pl.BlockSpec((td, nk), lambda b, pt: (b, 0)),
        scratch_shapes=[pltpu.VMEM((2, npages, IDX_DIM, page), F32), pltpu.SemaphoreType.DMA((2,))],
    )
    return pl.pallas_call(
        functools.partial(_sample_score_kernel, chunk_pages=chunk_pages),
        grid_spec=grid_spec, out_shape=jax.ShapeDtypeStruct((bd * td, nk), I32),
        compiler_params=_cparams(("arbitrary",)), name="sample_scores",
    )(page_table, qi, wi, kib, cache_idx_t)


def _select_kernel(key_ref, bias_ref, *, topk):
    rows, nk = key_ref.shape
    ntile = nk // LANES

    def count(pred):
        acc = jnp.zeros((rows, LANES), I32)
        for c in range(ntile):
            acc = acc + pred(key_ref[:, c * LANES:(c + 1) * LANES], c).astype(I32)
        return jnp.sum(acc, axis=-1, keepdims=True)

    t0 = jnp.where(count(lambda k, c: k >= 0) >= topk, 0, INT_MIN).astype(I32)

    def bit_body(p, t):
        cand = t | jnp.left_shift(jnp.int32(1), 30 - p)
        return jnp.where(count(lambda k, c: k >= cand) >= topk, cand, t)
    thr = lax.fori_loop(0, 31, bit_body, t0)

    n_gt = count(lambda k, c: k > thr)
    n_eq = count(lambda k, c: k == thr)
    need = topk - n_gt
    l_io = lax.broadcasted_iota(I32, (rows, LANES), 1)
    n_idx_bits = max(1, nk.bit_length())

    def tie_search(_):
        def body(p, m):
            cand = m | jnp.left_shift(jnp.int32(1), n_idx_bits - 1 - p)
            below = count(lambda k, c: (k == thr) & (l_io + c * LANES < cand))
            return jnp.where(below < need, cand, m)
        return lax.fori_loop(0, n_idx_bits, body, jnp.zeros((rows, 1), I32))

    excess = jnp.max(jnp.where((need > 0) & (thr > INT_MIN), n_eq - need, 0)) > 0
    idx_max = lax.cond(excess, tie_search, lambda _: jnp.full((rows, 1), nk, I32), 0)

    for c in range(ntile):
        k = key_ref[:, c * LANES:(c + 1) * LANES]
        sel = ((k > thr) | ((k == thr) & (l_io + c * LANES <= idx_max))) & (k != INT_MIN)
        bias_ref[:, c * LANES:(c + 1) * LANES] = jnp.where(sel, 0.0, NEG).astype(F32)


def _select(keys, topk, rows):
    n, nk = keys.shape
    return pl.pallas_call(
        functools.partial(_select_kernel, topk=topk),
        grid=(n // rows,),
        in_specs=[pl.BlockSpec((rows, nk), lambda i: (i, 0))],
        out_specs=pl.BlockSpec((rows, nk), lambda i: (i, 0)),
        out_shape=jax.ShapeDtypeStruct((n, nk), F32),
        compiler_params=_cparams(("parallel",)), name="sample_select",
    )(keys)


def _sample_attn_kernel(pt_ref, q_ref, bias_ref, kn_ref, vn_ref, ck_hbm, cv_hbm, a_ref,
                        kbuf, vbuf, s_ref, ksem, vsem, *, chunk_pages):
    b = pl.program_id(0)
    nb = pl.num_programs(0)
    slot = b % 2
    npages, page = kbuf.shape[1], kbuf.shape[4]
    td = q_ref.shape[0]
    half = GROUP * td

    @pl.when(b == 0)
    def _():
        _start_pages(pt_ref, b, ck_hbm, kbuf, 0, ksem)
        _start_pages(pt_ref, b, cv_hbm, vbuf, 0, vsem)

    @pl.when(b + 1 < nb)
    def _():
        _start_pages(pt_ref, b + 1, ck_hbm, kbuf, 1 - slot, ksem)
        _start_pages(pt_ref, b + 1, cv_hbm, vbuf, 1 - slot, vsem)

    q = _rows_by_head(q_ref[...], HEAD_DIM)
    qg = [q[g * half:(g + 1) * half] for g in range(N_KV_HEADS)]
    cw = chunk_pages * page
    nchunk = npages // chunk_pages
    new_lo = npages * page
    kn = _pad_rows(kn_ref[...], LANES).astype(BF16)
    vn = _pad_rows(vn_ref[...], LANES).astype(BF16)

    def chunk_t(buf, c, g):
        return jnp.concatenate([buf[slot, c * chunk_pages + j, g] for j in range(chunk_pages)],
                               axis=1).astype(BF16)

    def tile_bias(lo, n):
        return jnp.concatenate([bias_ref[:, lo:lo + n]] * GROUP, axis=0)

    _wait_pages(ck_hbm, kbuf, slot, ksem)
    ms = []
    for g in range(N_KV_HEADS):
        rows = slice(g * half, (g + 1) * half)
        mrun = jnp.full((half, LANES), NEG, F32)
        for c in range(nchunk):
            s = jnp.dot(qg[g], chunk_t(kbuf, c, g), preferred_element_type=F32) + tile_bias(c * cw, cw)
            s_ref[rows, c * cw:(c + 1) * cw] = s
            for j in range(cw // LANES):
                mrun = jnp.maximum(mrun, s[:, j * LANES:(j + 1) * LANES])
        s_new = lax.dot_general(qg[g], kn[:, g * HEAD_DIM:(g + 1) * HEAD_DIM], _NT,
                                preferred_element_type=F32) + tile_bias(new_lo, LANES)
        s_ref[rows, new_lo:] = s_new
        ms.append(jnp.max(jnp.maximum(mrun, s_new), axis=-1, keepdims=True))

    _wait_pages(cv_hbm, vbuf, slot, vsem)
    for g in range(N_KV_HEADS):
        rows = slice(g * half, (g + 1) * half)
        lrun = jnp.zeros((half, LANES), F32)
        acc = jnp.zeros((half, HEAD_DIM), F32)
        for c in range(nchunk):
            p = jnp.exp2(s_ref[rows, c * cw:(c + 1) * cw] - ms[g])
            for j in range(cw // LANES):
                lrun = lrun + p[:, j * LANES:(j + 1) * LANES]
            acc = acc + lax.dot_general(p.astype(BF16), chunk_t(vbuf, c, g), _NT,
                                        preferred_element_type=F32)
        p = jnp.exp2(s_ref[rows, new_lo:] - ms[g])
        lrun = lrun + p
        acc = acc + jnp.dot(p.astype(BF16), vn[:, g * HEAD_DIM:(g + 1) * HEAD_DIM],
                            preferred_element_type=F32)
        o = acc / jnp.sum(lrun, axis=-1, keepdims=True)
        for hh in range(GROUP):
            h = GROUP * g + hh
            a_ref[:, h * HEAD_DIM:(h + 1) * HEAD_DIM] = o[hh * td:(hh + 1) * td, :].astype(BF16)


def _sample_attention(page_table, q, bias, k_new, v_new, cache_kt, cache_vt, td, chunk_pages=4):
    bd, npages = page_table.shape
    page = cache_kt.shape[3]
    nk = bias.shape[1]
    rowb = lambda b, pt: (b, 0)
    grid_spec = pltpu.PrefetchScalarGridSpec(
        num_scalar_prefetch=1, grid=(bd,),
        in_specs=[
            pl.BlockSpec((td, ATTN_WIDTH), rowb),
            pl.BlockSpec((td, nk), rowb),
            pl.BlockSpec((td, KV_WIDTH), rowb),
            pl.BlockSpec((td, KV_WIDTH), rowb),
            pl.BlockSpec(memory_space=pl.ANY),
            pl.BlockSpec(memory_space=pl.ANY),
        ],
        out_specs=pl.BlockSpec((td, ATTN_WIDTH), rowb),
        scratch_shapes=[
            pltpu.VMEM((2, npages, N_KV_HEADS, HEAD_DIM, page), F32),
            pltpu.VMEM((2, npages, N_KV_HEADS, HEAD_DIM, page), F32),
            pltpu.VMEM((N_HEADS * td, nk), F32),
            pltpu.SemaphoreType.DMA((2,)),
            pltpu.SemaphoreType.DMA((2,)),
        ],
    )
    return pl.pallas_call(
        functools.partial(_sample_attn_kernel, chunk_pages=chunk_pages),
        grid_spec=grid_spec, out_shape=jax.ShapeDtypeStruct((bd * td, ATTN_WIDTH), BF16),
        compiler_params=_cparams(("arbitrary",)), name="sample_dsa",
    )(page_table, q, bias, k_new, v_new, cache_kt, cache_vt)


def _layer_weights(i, g_mix, w_in, g_idx_k, b_idx_k, w_dw, b_dw, g_conv_ln, b_conv_ln, w_out, g_ffn, w_up,
                   w_ffn_conv, b_ffn_conv, w_down, g_ple, w_ple_gate, w_ple, g_final):
    d = w_in.shape[1]
    w = w_in[i]
    n_qkvi = C_KI + IDX_DIM + IDX_HEADS
    pad = jnp.zeros((d, LANES - IDX_DIM - IDX_HEADS), w.dtype)
    w_comb = jnp.concatenate([w[:, :n_qkvi], pad, w[:, n_qkvi:]], axis=1).astype(BF16)
    zpad = jnp.zeros((LANES - IDX_DIM,), F32)
    gik = jnp.concatenate([g_idx_k[i], zpad])[None]
    bik = jnp.concatenate([b_idx_k[i], zpad])[None]
    proj = (g_mix[i][None], w_comb, gik, bik)

    wo = w_out[i].astype(BF16)
    conv = (w_dw[i], b_dw[i][None], g_conv_ln[i][None], b_conv_ln[i][None], wo[:ATTN_WIDTH], wo[ATTN_WIDTH:])

    d_ff = w_down.shape[1]
    nch = d_ff // FFN_CW
    ffn = (g_ffn[i][None], _ffn_chunks(w_up[i].astype(BF16), nch), _ffn_chunks(w_ffn_conv[i], nch),
           _ffn_chunks(b_ffn_conv[i][None], nch), w_down[i].astype(BF16).reshape(nch, FFN_CW, d),
           g_ple[i][None], w_ple_gate[i].astype(BF16), w_ple[i].astype(BF16), g_final[None])
    return proj, conv, ffn


def _ffn_chunks(m, nch):
    rows = m.shape[0]
    return m.reshape(rows, 2, nch, FFN_CW).transpose(2, 0, 1, 3).reshape(nch, rows, 2 * FFN_CW)


def _ffn_unchunk(st):
    nch, rows, cw2 = st.shape
    return st.reshape(nch, rows, 2, cw2 // 2).transpose(1, 2, 0, 3).reshape(rows, nch * cw2)


def kernel(x_prompt, x_sample, p_prompt, p_sample, cache_k, cache_v, cache_idx_k, state_conv, state_ffn_conv,
           page_table, g_mix, w_in, g_idx_k, b_idx_k, w_dw, b_dw, g_conv_ln, b_conv_ln, w_out, g_ffn, w_up,
           w_ffn_conv, b_ffn_conv, w_down, g_ple, w_ple_gate, w_ple, g_final):
    B, S, D = x_prompt.shape
    Bd, Td, _ = x_sample.shape
    depth = w_in.shape[0]
    n_pages = page_table.shape[1]
    page = cache_k.shape[2]
    past_len = n_pages * page
    d_ff = w_down.shape[1]
    nch = d_ff // FFN_CW
    topk_prompt = min(TOPK_MAX, S // 4)
    topk_sample = min(TOPK_MAX, (past_len + Td) // 4)
    assert S % KC == 0 and Td == SUBLANES and d_ff % FFN_CW == 0 and page == LANES

    tm_p = KC
    BL = min(32, Bd)
    G = Bd // BL
    tm_s = Td * BL

    tabs_p = _rope_tables(jnp.arange(S, dtype=I32))
    pos_s = past_len + jnp.arange(Td, dtype=I32)
    tabs_s = tuple(jnp.tile(t, (Bd, 1)) for t in _rope_tables(pos_s))

    def to_tm(a):
        return a.reshape(G, BL, Td, a.shape[-1]).transpose(0, 2, 1, 3).reshape(G * tm_s, a.shape[-1])

    def from_tm(a):
        return a.reshape(G, Td, BL, a.shape[-1]).transpose(0, 2, 1, 3).reshape(Bd, Td, a.shape[-1])

    hp = x_prompt.reshape(B * S, D)
    hs = x_sample.reshape(Bd * Td, D)
    outs = {k: [] for k in ("kp", "vp", "ip", "cp", "fp", "ks", "vs", "is", "cs", "fs")}
    for i in range(depth):
        last = i == depth - 1
        proj_w, conv_w, ffn_w = _layer_weights(
            i, g_mix, w_in, g_idx_k, b_idx_k, w_dw, b_dw, g_conv_ln, b_conv_ln, w_out, g_ffn, w_up,
            w_ffn_conv, b_ffn_conv, w_down, g_ple, w_ple_gate, w_ple, g_final)

        q, qit, kt, vt, kit, ktb, vb, kib, wit, glu = _project(
            hp, tabs_p, S // tm_p, *proj_w, tm=tm_p, prompt=True, batch=B)
        a = _prompt_attention(q, qit, wit, kib, ktb, vb, B, S, topk_prompt)
        nt = S // tm_p
        halo_rows = 32
        halo_spec = pl.BlockSpec(
            (halo_rows, CONV_CH), lambda b, t: (jnp.maximum((b * nt + t) * (tm_p // halo_rows) - 1, 0), 0))
        h1 = _conv_out(glu, glu, halo_spec, a, hp, *conv_w, groups=B, nt=nt, tm=tm_p, shift=1, zero_first=True)
        st0 = jnp.zeros((B, nch, SUBLANES, 2 * FFN_CW), F32)
        hp, st = _ffn(h1, p_prompt[i].reshape(B * S, -1), st0, ffn_w, groups=B, nt=nt, tm=tm_p,
                      shift=1, halo=SUBLANES, final=last)
        outs["kp"].append(kt.transpose(0, 3, 1, 2))
        outs["vp"].append(vt.transpose(0, 3, 1, 2))
        outs["ip"].append(kit.transpose(0, 2, 1))
        outs["cp"].append(glu.reshape(B, S, CONV_CH)[:, S - (CONV_W - 1):])
        outs["fp"].append(jax.vmap(_ffn_unchunk)(st)[:, SUBLANES - (FFN_CONV_W - 1):])

        tm_sp = min(512, Bd * Td)
        q, qi, k, v, ki, kib, wi, glu = _project(hs, tabs_s, Bd * Td // tm_sp, *proj_w, tm=tm_sp, prompt=False)
        keys = _sample_scores(page_table, qi, wi, kib, cache_idx_k[i].transpose(0, 2, 1), Td)
        bias = _select(keys, topk_sample, rows=min(64, Bd * Td))
        a = _sample_attention(page_table, q, bias, k, v,
                              cache_k[i].transpose(0, 2, 3, 1), cache_v[i].transpose(0, 2, 3, 1), Td)
        sc = state_conv[i]
        halo_s = sc.reshape(G, BL, CONV_W - 1, CONV_CH).transpose(0, 2, 1, 3).reshape(G * (CONV_W - 1) * BL, CONV_CH)
        halo_spec = pl.BlockSpec(((CONV_W - 1) * BL, CONV_CH), lambda g, t: (g, 0))
        h1 = _conv_out(to_tm(glu), halo_s, halo_spec, to_tm(a), to_tm(hs), *conv_w, groups=G, nt=1, tm=tm_s,
                       shift=BL, zero_first=False)
        sf = state_ffn_conv[i]
        sf = sf.reshape(G, BL, FFN_CONV_W - 1, 2 * d_ff).transpose(0, 2, 1, 3).reshape(G, (FFN_CONV_W - 1) * BL, 2 * d_ff)
        st0 = jax.vmap(lambda s: _ffn_chunks(s, nch))(sf)
        hs_tm, st = _ffn(h1, to_tm(p_sample[i].reshape(Bd * Td, -1)), st0, ffn_w, groups=G, nt=1, tm=tm_s,
                         shift=BL, halo=(FFN_CONV_W - 1) * BL, final=last)
        hs = from_tm(hs_tm).reshape(Bd * Td, D)
        fs = jax.vmap(_ffn_unchunk)(st)
        fs = fs.reshape(G, FFN_CONV_W - 1, BL, 2 * d_ff).transpose(0, 2, 1, 3).reshape(Bd, FFN_CONV_W - 1, 2 * d_ff)
        outs["ks"].append(k.reshape(Bd, Td, N_KV_HEADS, HEAD_DIM))
        outs["vs"].append(v.reshape(Bd, Td, N_KV_HEADS, HEAD_DIM))
        outs["is"].append(ki.reshape(Bd, Td, IDX_DIM))
        outs["cs"].append(jnp.concatenate([sc, glu.reshape(Bd, Td, CONV_CH)], axis=1)[:, Td:])
        outs["fs"].append(fs)

    y_prompt = hp.reshape(B, S, D)
    y_sample = hs.reshape(Bd, Td, D)
    st = lambda name: jnp.stack(outs[name])
    return (y_prompt, y_sample, st("kp"), st("vp"), st("ip"), st("cp"), st("fp"),
            st("ks"), st("vs"), st("is"), st("cs"), st("fs"))
```

```python
import functools

import jax
import jax.numpy as jnp
from jax import lax
from jax.experimental import pallas as pl
from jax.experimental.pallas import tpu as pltpu

N_HEADS = 8
HEAD_DIM = 64
N_KV_HEADS = 2
GROUP = N_HEADS // N_KV_HEADS
ATTN_WIDTH = N_HEADS * HEAD_DIM
KV_WIDTH = N_KV_HEADS * HEAD_DIM
IDX_HEADS = 8
IDX_DIM = 64
IDX_ROPE_DIM = 32
TOPK_MAX = 256
CONV_W = 31
FFN_CONV_W = 3
ROPE_THETA = 10000.0
EPS = 1e-6
NEG = -1e30

LANES = 128
SUBLANES = 8
INT_MIN = -2 ** 31
VMEM_LIMIT = 56 * 1024 * 1024

TQ = LANES
TPC = 4
KC = TPC * LANES
CPC = TPC
Q_SCALE = HEAD_DIM ** -0.5 * 1.4426950408889634
BISECT_FLOAT_PASSES = 16
BISECT_MIN_PASSES = 16
BISECT_CHECK_EVERY = 4
BISECT_MAX_PASSES = 48

F32 = jnp.float32
BF16 = jnp.bfloat16
I32 = jnp.int32

_NT = (((1,), (1,)), ((), ()))


def _cparams(sem):
    return pltpu.CompilerParams(dimension_semantics=sem, vmem_limit_bytes=VMEM_LIMIT)


def _sortable(x):
    b = pltpu.bitcast(x, I32)
    return b ^ ((b >> 31) & jnp.int32(0x7FFFFFFF))


def _rmsnorm(x, g):
    return x * lax.rsqrt(jnp.mean(x * x, axis=-1, keepdims=True) + EPS) * g


C_Q = 0
C_K = C_Q + ATTN_WIDTH
C_V = C_K + KV_WIDTH
C_QI = C_V + KV_WIDTH
C_KI = C_QI + IDX_HEADS * IDX_DIM
C_A = C_KI + LANES
CONV_CH = 512
C_G = C_A + CONV_CH
C_END = C_G + CONV_CH


def _proj_kernel(x_ref, g_ref, w_ref, gik_ref, bik_ref, cq_ref, sq_ref, ci_ref, si_ref, *out_refs,
                 wi_scale, prompt):
    tm = x_ref.shape[0]
    hn = _rmsnorm(x_ref[...], g_ref[...]).astype(BF16)
    z = jnp.dot(hn, w_ref[...], preferred_element_type=F32)

    lane = lax.broadcasted_iota(I32, (tm, LANES), 1)
    in_head = lane % HEAD_DIM
    cq, sq, ci, si = cq_ref[...], sq_ref[...], ci_ref[...], si_ref[...]

    def rope_full(xg):
        sw = jnp.where(in_head < HEAD_DIM // 2, pltpu.roll(xg, LANES - HEAD_DIM // 2, 1),
                       pltpu.roll(xg, HEAD_DIM // 2, 1))
        return xg * cq + sw * sq

    def rope_part(xg):
        sw = jnp.where(in_head < IDX_ROPE_DIM // 2, pltpu.roll(xg, LANES - IDX_ROPE_DIM // 2, 1),
                       pltpu.roll(xg, IDX_ROPE_DIM // 2, 1))
        return xg * ci + sw * si

    q_groups = [rope_full(z[:, C_Q + g * LANES:C_Q + (g + 1) * LANES]) * Q_SCALE
                for g in range(ATTN_WIDTH // LANES)]
    qi_groups = [rope_part(z[:, C_QI + g * LANES:C_QI + (g + 1) * LANES])
                 for g in range(IDX_HEADS * IDX_DIM // LANES)]
    kr = rope_full(z[:, C_K:C_K + LANES])
    vr = z[:, C_V:C_V + LANES]

    zg = z[:, C_KI:C_KI + LANES]
    is_ki = lane < IDX_DIM
    mu = jnp.sum(jnp.where(is_ki, zg, 0.0), axis=-1, keepdims=True) / IDX_DIM
    xc = jnp.where(is_ki, zg - mu, 0.0)
    var = jnp.sum(xc * xc, axis=-1, keepdims=True) / IDX_DIM
    kin = rope_part(xc * lax.rsqrt(var + EPS) * gik_ref[...] + bik_ref[...])
    wig = zg * wi_scale
    glu = z[:, C_A:C_A + CONV_CH] * jax.nn.sigmoid(z[:, C_G:C_G + CONV_CH])

    if prompt:
        q_ref, qit_ref, kt_ref, vt_ref, kit_ref, ktb_ref, vb_ref, kib_ref, wit_ref, glu_ref = out_refs
        for g, qg in enumerate(q_groups):
            q_ref[:, g * LANES:(g + 1) * LANES] = qg.astype(BF16)
        for g, qig in enumerate(qi_groups):
            qit_ref[g * LANES:(g + 1) * LANES, :] = qig.T.astype(BF16)
        krt = kr.T
        vrt = vr.T
        for g in range(N_KV_HEADS):
            rows = slice(g * HEAD_DIM, (g + 1) * HEAD_DIM)
            kt_ref[0, g] = krt[rows]
            vt_ref[0, g] = vrt[rows]
            ktb_ref[0, g] = krt[rows].astype(BF16)
            vg = vr if g == 0 else pltpu.roll(vr, LANES - g * HEAD_DIM, 1)
            vb_ref[g] = jnp.where(lane < HEAD_DIM, vg, jnp.where(lane == HEAD_DIM, 1.0, 0.0)).astype(BF16)
        kit_ref[0] = kin.T[:IDX_DIM]
        kib_ref[...] = kin[:, :IDX_DIM].astype(BF16)
        wit_ref[...] = wig.T[IDX_DIM:IDX_DIM + IDX_HEADS, :]
        glu_ref[...] = glu
    else:
        q_ref, qi_ref, k_ref, v_ref, ki_ref, kib_ref, wi_ref, glu_ref = out_refs
        for g, qg in enumerate(q_groups):
            q_ref[:, g * LANES:(g + 1) * LANES] = qg.astype(BF16)
        for g, qig in enumerate(qi_groups):
            qi_ref[:, g * LANES:(g + 1) * LANES] = qig.astype(BF16)
        k_ref[...] = kr
        v_ref[...] = vr
        ki_ref[...] = kin[:, :IDX_DIM]
        kib_ref[...] = kin[:, :IDX_DIM].astype(BF16)
        wi_ref[...] = wig[:, IDX_DIM:IDX_DIM + IDX_HEADS]
        glu_ref[...] = glu


def _rope_tables(pos):
    def cs(half):
        inv = jnp.power(jnp.float32(ROPE_THETA), -jnp.arange(half, dtype=F32) / half)
        ang = pos.astype(F32)[:, None] * inv[None, :]
        return jnp.cos(ang), jnp.sin(ang)

    c, s = cs(HEAD_DIM // 2)
    cq = jnp.concatenate([c, c, c, c], axis=-1)
    sq = jnp.concatenate([-s, s, -s, s], axis=-1)
    c2, s2 = cs(IDX_ROPE_DIM // 2)
    one = jnp.ones((pos.shape[0], IDX_DIM - IDX_ROPE_DIM), F32)
    ci = jnp.concatenate([c2, c2, one, c2, c2, one], axis=-1)
    si = jnp.concatenate([-s2, s2, 0 * one, -s2, s2, 0 * one], axis=-1)
    return cq, sq, ci, si


def _project(x, tables, tab_blocks, g_mix, w_comb, gik, bik, *, tm, prompt, batch=None):
    n, d = x.shape
    nb = n // tm
    row = lambda i: (i, 0)
    const = lambda i: (0, 0)
    tab = lambda i: (i % tab_blocks, 0)
    wi_scale = IDX_HEADS ** -0.5 * IDX_DIM ** -0.5
    sds = jax.ShapeDtypeStruct
    if prompt:
        seq = n // batch
        nt = seq // tm
        bt = lambda i: (i // nt, 0, 0, i % nt)
        out_shape = (
            sds((n, ATTN_WIDTH), BF16),
            sds((IDX_HEADS * IDX_DIM, n), BF16),
            sds((batch, N_KV_HEADS, HEAD_DIM, seq), F32),
            sds((batch, N_KV_HEADS, HEAD_DIM, seq), F32),
            sds((batch, IDX_DIM, seq), F32),
            sds((nb, N_KV_HEADS, HEAD_DIM, tm), BF16),
            sds((N_KV_HEADS, n, LANES), BF16),
            sds((n, IDX_DIM), BF16),
            sds((IDX_HEADS, n), F32),
            sds((n, CONV_CH), F32),
        )
        out_specs = (
            pl.BlockSpec((tm, ATTN_WIDTH), row),
            pl.BlockSpec((IDX_HEADS * IDX_DIM, tm), lambda i: (0, i)),
            pl.BlockSpec((1, N_KV_HEADS, HEAD_DIM, tm), bt),
            pl.BlockSpec((1, N_KV_HEADS, HEAD_DIM, tm), bt),
            pl.BlockSpec((1, IDX_DIM, tm), lambda i: (i // nt, 0, i % nt)),
            pl.BlockSpec((1, N_KV_HEADS, HEAD_DIM, tm), lambda i: (i, 0, 0, 0)),
            pl.BlockSpec((N_KV_HEADS, tm, LANES), lambda i: (0, i, 0)),
            pl.BlockSpec((tm, IDX_DIM), row),
            pl.BlockSpec((IDX_HEADS, tm), lambda i: (0, i)),
            pl.BlockSpec((tm, CONV_CH), row),
        )
    else:
        out_shape = (
            sds((n, ATTN_WIDTH), BF16),
            sds((n, IDX_HEADS * IDX_DIM), BF16),
            sds((n, KV_WIDTH), F32),
            sds((n, KV_WIDTH), F32),
            sds((n, IDX_DIM), F32),
            sds((n, IDX_DIM), BF16),
            sds((n, IDX_HEADS), F32),
            sds((n, CONV_CH), F32),
        )
        out_specs = (
            pl.BlockSpec((tm, ATTN_WIDTH), row),
            pl.BlockSpec((tm, IDX_HEADS * IDX_DIM), row),
            pl.BlockSpec((tm, KV_WIDTH), row),
            pl.BlockSpec((tm, KV_WIDTH), row),
            pl.BlockSpec((tm, IDX_DIM), row),
            pl.BlockSpec((tm, IDX_DIM), row),
            pl.BlockSpec((tm, IDX_HEADS), row),
            pl.BlockSpec((tm, CONV_CH), row),
        )
    in_specs = [
        pl.BlockSpec((tm, d), row),
        pl.BlockSpec((1, d), const),
        pl.BlockSpec((d, C_END), const),
        pl.BlockSpec((1, LANES), const),
        pl.BlockSpec((1, LANES), const),
    ] + [pl.BlockSpec((tm, LANES), tab)] * 4
    return pl.pallas_call(
        functools.partial(_proj_kernel, wi_scale=wi_scale, prompt=prompt),
        grid=(nb,), in_specs=in_specs, out_specs=out_specs, out_shape=out_shape,
        compiler_params=_cparams(("parallel",)), name="in_proj",
    )(x, g_mix, w_comb, gik, bik, *tables)


def _unsort(k):
    return pltpu.bitcast(k ^ ((k >> 31) & jnp.int32(0x7FFFFFFF)), F32)


def _find_separator(count_ge, lo0, hi0, n_keys, topk):
    enough = n_keys >= topk

    def one_pass(p, lo, hi, cnt):
        mid_i = (lo >> 1) + (hi >> 1) + (lo & hi & 1)
        mid_f = _sortable(0.5 * _unsort(lo) + 0.5 * _unsort(hi))
        use_f = (p < BISECT_FLOAT_PASSES) & (mid_f > lo) & (mid_f < hi)
        mid = jnp.where(use_f, mid_f, mid_i)
        c = count_ge(mid)
        ge = c >= topk
        return jnp.where(ge, mid, lo), jnp.where(ge, hi, mid), jnp.where(ge, c, cnt)

    def cond(st):
        p, done = st[0], st[1]
        return jnp.logical_and(jnp.logical_not(done), p < BISECT_MAX_PASSES)

    def body(st):
        p, _, lo, hi, cnt = st
        for u in range(BISECT_CHECK_EVERY):
            lo, hi, cnt = one_pass(p + u, lo, hi, cnt)
        settled = (cnt == topk) | (lo + 1 >= hi) | jnp.logical_not(enough)
        return p + BISECT_CHECK_EVERY, jnp.min(settled.astype(I32)) > 0, lo, hi, cnt

    lo1, hi1, cnt1 = lax.fori_loop(0, BISECT_MIN_PASSES, lambda p, st: one_pass(p, *st), (lo0, hi0, n_keys))
    _, _, lo, _, _ = lax.while_loop(cond, body, (jnp.int32(BISECT_MIN_PASSES), jnp.bool_(False), lo1, hi1, cnt1))
    return jnp.where(enough, lo, INT_MIN)


def _tile_count(pred):
    return jnp.sum(pred.astype(I32).reshape(LANES // SUBLANES, SUBLANES, LANES), axis=0)


def _pair_loop(n, body, init):
    carry = lax.fori_loop(0, n // 2, lambda k, cr: body(2 * k + 1, body(2 * k, cr)), init)
    return lax.cond(n % 2 == 1, lambda cr: body(n - 1, cr), lambda cr: cr, carry)


def _count_keys(key_ref, nsteps, pred):
    def body(c, accs):
        return tuple(a + _tile_count(pred(key_ref[c * CPC + j], c * CPC + j)) for j, a in enumerate(accs))
    z8 = jnp.zeros((SUBLANES, LANES), I32)
    accs = lax.fori_loop(0, nsteps, body, (z8,) * CPC)
    return jnp.sum(sum(accs[1:], accs[0]), axis=0, keepdims=True)


def _prompt_attn_kernel(qit_ref, wit_ref, q_ref, kib_ref, ktb_ref, vb_ref, a_ref,
                        key_ref, bias_ref, s_ref, m_ref, acc_ref, *, topk):
    i = pl.program_id(1)
    nch = i // TPC + 1
    r_io = lax.broadcasted_iota(I32, (TQ, TQ), 0)
    c_io = lax.broadcasted_iota(I32, (TQ, TQ), 1)

    def causal(kt):
        return r_io + (kt - i) * TQ <= c_io

    qit = jnp.concatenate([qit_ref[h * IDX_DIM:(h + 1) * IDX_DIM, :] for h in range(IDX_HEADS)], axis=1)
    w_rows = [wit_ref[h:h + 1, :] for h in range(IDX_HEADS)]

    def score_body(c, mm):
        kmin, kmax = mm
        for j in range(TPC):
            kt = c * TPC + j
            kit = kib_ref[pl.ds(pl.multiple_of(kt * TQ, TQ), TQ), :]
            lg = jnp.dot(kit, qit, preferred_element_type=F32)
            sc = jnp.maximum(lg[:, :TQ], 0.0) * w_rows[0]
            for h in range(1, IDX_HEADS):
                sc = sc + jnp.maximum(lg[:, h * TQ:(h + 1) * TQ], 0.0) * w_rows[h]
            vis = causal(kt)
            k = _sortable(sc)
            key_ref[kt] = jnp.where(vis, k, INT_MIN)
            kmax = jnp.maximum(kmax, jnp.where(vis, k, INT_MIN))
            kmin = jnp.minimum(kmin, jnp.where(vis, k, jnp.int32(2 ** 31 - 1)))
        return kmin, kmax
    kmin, kmax = _pair_loop(nch, score_body, (jnp.full((TQ, TQ), 2 ** 31 - 1, I32),
                                              jnp.full((TQ, TQ), INT_MIN, I32)))

    n_keys = i * TQ + c_io[0:1, :] + 1
    lo0 = jnp.min(kmin, axis=0, keepdims=True)
    hi0 = jnp.max(kmax, axis=0, keepdims=True) + 1

    thr = _find_separator(lambda mid: _count_keys(key_ref, nch, lambda k, kt: k >= mid),
                          lo0, hi0, n_keys, topk)

    n_gt = _count_keys(key_ref, nch, lambda k, kt: k > thr)
    n_eq = _count_keys(key_ref, nch, lambda k, kt: k == thr)
    need = topk - n_gt
    n_idx_bits = max(1, (key_ref.shape[0] * TQ).bit_length())

    def tie_search(_):
        def body(p, m):
            cand = m | jnp.left_shift(jnp.int32(1), n_idx_bits - 1 - p)
            below = _count_keys(key_ref, nch, lambda k, kt: (k == thr) & (r_io + kt * TQ < cand))
            return jnp.where(below < need, cand, m)
        return lax.fori_loop(0, n_idx_bits, body, jnp.zeros((1, TQ), I32))

    excess = jnp.max(jnp.where((need > 0) & (thr > INT_MIN), n_eq - need, 0)) > 0
    idx_max = lax.cond(excess, tie_search,
                       lambda _: jnp.full((1, TQ), key_ref.shape[0] * TQ, I32), 0)

    def write_bias(c):
        for j in range(TPC):
            kt = c * TPC + j
            k = key_ref[kt]
            sel = ((k > thr) | ((k == thr) & (r_io + kt * TQ <= idx_max))) & causal(kt)
            bias_ref[kt] = jnp.where(sel, 0.0, NEG).astype(F32).T.astype(BF16)
    last = nch - 1
    write_bias(0)
    write_bias(jnp.minimum(1, last))

    q = q_ref[...]
    row_q = lax.broadcasted_iota(I32, (GROUP * TQ, TQ), 0) % TQ
    onehot = (row_q == lax.broadcasted_iota(I32, (GROUP * TQ, TQ), 1)).astype(BF16)
    lhs = [jnp.concatenate([onehot] + [jnp.concatenate(
        [q[:, (GROUP * g + hh) * HEAD_DIM:(GROUP * g + hh + 1) * HEAD_DIM] for hh in range(GROUP)], axis=0)],
        axis=1) for g in range(N_KV_HEADS)]
    m_ref[...] = jnp.full(m_ref.shape, NEG, F32)

    def qk_chunk(c):
        bias = jnp.concatenate([bias_ref[c * TPC + j] for j in range(TPC)], axis=1)
        for g in range(N_KV_HEADS):
            rhs = jnp.concatenate([bias, ktb_ref[c, g]], axis=0)
            s = jnp.dot(lhs[g], rhs, preferred_element_type=F32)
            s_ref[g, c] = s
            cm = s[:, :TQ]
            for j in range(1, TPC):
                cm = jnp.maximum(cm, s[:, j * TQ:(j + 1) * TQ])
            m_ref[g] = jnp.maximum(m_ref[g], cm)

    def qk_pair(k, carry):
        qk_chunk(2 * k)
        qk_chunk(2 * k + 1)
        write_bias(jnp.minimum(2 * k + 2, last))
        write_bias(jnp.minimum(2 * k + 3, last))
        return carry
    lax.fori_loop(0, nch // 2, qk_pair, 0)

    @pl.when(nch % 2 == 1)
    def _():
        qk_chunk(last)
    m = [jnp.max(m_ref[g], axis=-1, keepdims=True) for g in range(N_KV_HEADS)]
    acc_ref[...] = jnp.zeros(acc_ref.shape, F32)

    def pv_chunk(g, c):
        p = jnp.exp2(s_ref[g, c] - m[g]).astype(BF16)
        vc = vb_ref[g, pl.ds(pl.multiple_of(c * KC, KC), KC), :]
        return jnp.dot(p, vc, preferred_element_type=F32)

    def pv_pair(k, carry):
        for g in range(N_KV_HEADS):
            acc_ref[g] += pv_chunk(g, 2 * k) + pv_chunk(g, 2 * k + 1)
        return carry
    lax.fori_loop(0, nch // 2, pv_pair, 0)

    @pl.when(nch % 2 == 1)
    def _():
        for g in range(N_KV_HEADS):
            acc_ref[g] += pv_chunk(g, last)
    for g in range(N_KV_HEADS):
        acc = acc_ref[g]
        o = acc[:, :HEAD_DIM] / acc[:, HEAD_DIM:HEAD_DIM + 1]
        for hh in range(GROUP):
            h = GROUP * g + hh
            a_ref[:, h * HEAD_DIM:(h + 1) * HEAD_DIM] = o[hh * TQ:(hh + 1) * TQ, :].astype(BF16)


def _prompt_attention(q, qit, wit, kib, ktb, vb, batch, seq, topk):
    nb = seq // TQ
    ncs = seq // KC
    qrow = lambda b, i: (b * nb + i, 0)
    qcol = lambda b, i: (0, b * nb + i)
    return pl.pallas_call(
        functools.partial(_prompt_attn_kernel, topk=topk),
        grid=(batch, nb),
        in_specs=[
            pl.BlockSpec((IDX_HEADS * IDX_DIM, TQ), qcol),
            pl.BlockSpec((IDX_HEADS, TQ), qcol),
            pl.BlockSpec((TQ, ATTN_WIDTH), qrow),
            pl.BlockSpec((seq, IDX_DIM), lambda b, i: (b, 0)),
            pl.BlockSpec((ncs, N_KV_HEADS, HEAD_DIM, KC), lambda b, i: (b, 0, 0, 0)),
            pl.BlockSpec((N_KV_HEADS, seq, LANES), lambda b, i: (0, b, 0)),
        ],
        out_specs=pl.BlockSpec((TQ, ATTN_WIDTH), qrow),
        out_shape=jax.ShapeDtypeStruct((batch * seq, ATTN_WIDTH), BF16),
        scratch_shapes=[
            pltpu.VMEM((nb, TQ, TQ), I32),
            pltpu.VMEM((nb, TQ, TQ), BF16),
            pltpu.VMEM((N_KV_HEADS, ncs, GROUP * TQ, KC), F32),
            pltpu.VMEM((N_KV_HEADS, GROUP * TQ, TQ), F32),
            pltpu.VMEM((N_KV_HEADS, GROUP * TQ, LANES), F32),
        ],
        compiler_params=_cparams(("parallel", "arbitrary")), name="prompt_dsa",
    )(qit, wit, q, kib, ktb, vb)


def _conv_out_kernel(glu_ref, halo_ref, a_ref, x_ref, wdw_ref, bdw_ref, gln_ref, bln_ref, wa_ref, wc_ref,
                     h_ref, xp_ref, c_ref, *maybe_xs_ref, shift, zero_first, rc):
    tm = glu_ref.shape[0]
    halo = halo_ref.shape[0]
    hv = halo_ref[...]
    if zero_first:
        hv = jnp.where(pl.program_id(1) == 0, 0.0, hv)
    xp_ref[0:halo, :] = hv
    xp_ref[halo:halo + tm, :] = glu_ref[...]
    if shift % SUBLANES:
        (xs_ref,) = maybe_xs_ref
        for r in range(1, SUBLANES):
            xs_ref[r - 1] = xp_ref[r:r + xs_ref.shape[1], :]

    def window(off, ls):
        r = off % SUBLANES
        if r == 0:
            return xp_ref[off:off + rc, ls]
        return xs_ref[r - 1, off - r:off - r + rc, ls]

    for lg in range(CONV_CH // LANES):
        ls = slice(lg * LANES, (lg + 1) * LANES)
        for r0 in range(0, tm, rc):
            acc = jnp.zeros((rc, LANES), F32)
            for j in range(CONV_W):
                off = halo - (CONV_W - 1 - j) * shift
                acc = acc + wdw_ref[j:j + 1, ls] * window(r0 + off, ls)
            c_ref[r0:r0 + rc, ls] = acc + bdw_ref[:, ls]

    c = c_ref[...]
    mu = jnp.mean(c, axis=-1, keepdims=True)
    xc = c - mu
    var = jnp.mean(xc * xc, axis=-1, keepdims=True)
    y = xc * lax.rsqrt(var + EPS) * gln_ref[...] + bln_ref[...]
    y = jax.nn.silu(y)
    h_ref[...] = (x_ref[...]
                  + jnp.dot(a_ref[...], wa_ref[...], preferred_element_type=F32)
                  + jnp.dot(y.astype(BF16), wc_ref[...], preferred_element_type=F32))


def _conv_out(glu, halo_arr, halo_spec, a, x, w_dw, b_dw, g_ln, b_ln, w_a, w_c, *, groups, nt, tm,
              shift, zero_first):
    d = x.shape[1]
    halo = halo_spec.block_shape[0]
    row = lambda b, i: (b * nt + i, 0)
    const = lambda b, i: (0, 0)
    return pl.pallas_call(
        functools.partial(_conv_out_kernel, shift=shift, zero_first=zero_first, rc=32),
        grid=(groups, nt),
        in_specs=[
            pl.BlockSpec((tm, CONV_CH), row),
            halo_spec,
            pl.BlockSpec((tm, ATTN_WIDTH), row),
            pl.BlockSpec((tm, d), row),
            pl.BlockSpec((CONV_W, CONV_CH), const),
            pl.BlockSpec((1, CONV_CH), const),
            pl.BlockSpec((1, CONV_CH), const),
            pl.BlockSpec((1, CONV_CH), const),
            pl.BlockSpec((ATTN_WIDTH, d), const),
            pl.BlockSpec((CONV_CH, d), const),
        ],
        out_specs=pl.BlockSpec((tm, d), row),
        out_shape=jax.ShapeDtypeStruct(x.shape, F32),
        scratch_shapes=[pltpu.VMEM((halo + tm, CONV_CH), F32), pltpu.VMEM((tm, CONV_CH), F32)]
        + ([pltpu.VMEM((SUBLANES - 1, halo + tm - SUBLANES, CONV_CH), F32)] if shift % SUBLANES else []),
        compiler_params=_cparams(("parallel", "arbitrary")), name="conv_out_proj",
    )(glu, halo_arr, a, x, w_dw, b_dw, g_ln, b_ln, w_a, w_c)


FFN_CW = 256
FFN_RC = 64
FFN_NBUF = 3
FFN_AHEAD = FFN_NBUF - 1


def _ffn_kernel(h_ref, p_ref, st_ref, gffn_ref, wu_ref, wf_ref, bf_ref, wdn_ref, gple_ref, wgate_ref, wple_ref,
                gfin_ref, y_ref, sto_ref, u_ref, act_ref, carry_ref, hn_ref, *, shift, halo, final):
    tm = h_ref.shape[0]
    nch = wu_ref.shape[0]
    cw = wu_ref.shape[2] // 2
    first = pl.program_id(1) == 0
    h = h_ref[...]
    hn_ref[...] = _rmsnorm(h, gffn_ref[...]).astype(BF16)

    @pl.when(first)
    def _():
        carry_ref[...] = st_ref[...]

    def up(c, buf):
        ub = u_ref.at[buf]
        ub[0:halo, :] = carry_ref[c]
        ub[halo:halo + tm, :] = jnp.dot(hn_ref[...], wu_ref[c], preferred_element_type=F32)
        tail = ub[tm:tm + halo, :]
        carry_ref[c] = tail
        sto_ref[c] = tail

    def act(c, buf):
        wf = wf_ref[c]
        bf = bf_ref[c]
        def conv(r0, ls):
            out = bf[:, ls]
            for k in range(FFN_CONV_W):
                off = r0 + halo - (FFN_CONV_W - 1 - k) * shift
                out = out + wf[k:k + 1, ls] * u_ref[buf, off:off + FFN_RC, ls]
            return out
        for r0 in range(0, tm, FFN_RC):
            for lg in range(cw // LANES):
                gate = conv(r0, slice(lg * LANES, (lg + 1) * LANES))
                val = conv(r0, slice(cw + lg * LANES, cw + (lg + 1) * LANES))
                act_ref[c, r0:r0 + FFN_RC, lg * LANES:(lg + 1) * LANES] = (jax.nn.silu(gate) * val).astype(BF16)

    for c in range(min(FFN_AHEAD, nch)):
        up(c, c % FFN_NBUF)
    n_trips = max(nch - FFN_AHEAD, 0) // FFN_NBUF

    def trip(k, carry):
        for j in range(FFN_NBUF):
            up(FFN_NBUF * k + j + FFN_AHEAD, (j + FFN_AHEAD) % FFN_NBUF)
            act(FFN_NBUF * k + j, j)
        return carry
    lax.fori_loop(0, n_trips, trip, 0)
    ups_done = min(nch, FFN_NBUF * n_trips + FFN_AHEAD)
    for c in range(FFN_NBUF * n_trips, nch):
        if ups_done <= c + FFN_AHEAD < nch:
            up(c + FFN_AHEAD, (c + FFN_AHEAD) % FFN_NBUF)
        act(c, c % FFN_NBUF)

    down = jnp.dot(act_ref[0], wdn_ref[0], preferred_element_type=F32)
    for c in range(1, nch):
        down = down + jnp.dot(act_ref[c], wdn_ref[c], preferred_element_type=F32)
    h2 = h + down
    gate = jax.nn.sigmoid(jnp.dot(_rmsnorm(h2, gple_ref[...]).astype(BF16), wgate_ref[...],
                                  preferred_element_type=F32))
    ple = jnp.dot(p_ref[...].astype(BF16), wple_ref[...], preferred_element_type=F32)
    h3 = h2 + ple * gate
    y_ref[...] = _rmsnorm(h3, gfin_ref[...]) if final else h3


def _ffn(h, p, st, wts, *, groups, nt, tm, shift, halo, final):
    (g_ffn, wu, wf, bf, wdn, g_ple, w_gate, w_ple, g_fin) = wts
    d = h.shape[1]
    nch, _, cw2 = wu.shape
    row = lambda b, i: (b * nt + i, 0)
    c2 = lambda b, i: (0, 0)
    c3 = lambda b, i: (0, 0, 0)
    stspec = pl.BlockSpec((None, nch, halo, cw2), lambda b, i: (b, 0, 0, 0))
    once = dict(pipeline_mode=pl.Buffered(1))
    return pl.pallas_call(
        functools.partial(_ffn_kernel, shift=shift, halo=halo, final=final),
        grid=(groups, nt),
        in_specs=[
            pl.BlockSpec((tm, d), row),
            pl.BlockSpec((tm, p.shape[1]), row),
            stspec,
            pl.BlockSpec((1, d), c2),
            pl.BlockSpec(wu.shape, c3, **once),
            pl.BlockSpec(wf.shape, c3),
            pl.BlockSpec(bf.shape, c3),
            pl.BlockSpec(wdn.shape, c3, **once),
            pl.BlockSpec((1, d), c2),
            pl.BlockSpec(w_gate.shape, c2, **once),
            pl.BlockSpec(w_ple.shape, c2, **once),
            pl.BlockSpec((1, d), c2),
        ],
        out_specs=(pl.BlockSpec((tm, d), row), stspec),
        out_shape=(jax.ShapeDtypeStruct(h.shape, F32),
                   jax.ShapeDtypeStruct((groups, nch, halo, cw2), F32)),
        scratch_shapes=[
            pltpu.VMEM((FFN_NBUF, halo + tm, cw2), F32),
            pltpu.VMEM((nch, tm, cw2 // 2), BF16),
            pltpu.VMEM((nch, halo, cw2), F32),
            pltpu.VMEM((tm, d), BF16),
        ],
        compiler_params=_cparams(("parallel", "arbitrary")), name="conv_ffn_ple",
    )(h, p, st, g_ffn, wu, wf, bf, wdn, g_ple, w_gate, w_ple, g_fin)


def _start_pages(pt_ref, b, src_hbm, dst_ref, slot, sem):
    for p in range(dst_ref.shape[1]):
        pltpu.make_async_copy(src_hbm.at[pt_ref[b, p]], dst_ref.at[slot, p], sem.at[slot]).start()


def _wait_pages(src_hbm, dst_ref, slot, sem):
    for p in range(dst_ref.shape[1]):
        pltpu.make_async_copy(src_hbm.at[0], dst_ref.at[slot, p], sem.at[slot]).wait()


def _rows_by_head(x, width):
    return jnp.concatenate([x[:, h * width:(h + 1) * width] for h in range(x.shape[1] // width)], axis=0)


def _pad_rows(x, rows):
    return jnp.concatenate([x, jnp.zeros((rows - x.shape[0], x.shape[1]), x.dtype)], axis=0)


def _sample_score_kernel(pt_ref, qi_ref, wi_ref, kin_ref, cidx_hbm, key_ref, ibuf, sem, *, chunk_pages):
    b = pl.program_id(0)
    nb = pl.num_programs(0)
    slot = b % 2
    npages, page = ibuf.shape[1], ibuf.shape[3]
    td = qi_ref.shape[0]

    @pl.when(b == 0)
    def _():
        _start_pages(pt_ref, b, cidx_hbm, ibuf, 0, sem)

    @pl.when(b + 1 < nb)
    def _():
        _start_pages(pt_ref, b + 1, cidx_hbm, ibuf, 1 - slot, sem)

    qi = _rows_by_head(qi_ref[...], IDX_DIM)
    wcol = jnp.concatenate([wi_ref[:, h:h + 1] for h in range(IDX_HEADS)], axis=0)

    def head_sum(lg):
        r = jnp.maximum(lg, 0.0) * wcol
        return jnp.sum(r.reshape(IDX_HEADS, td, r.shape[1]), axis=0)

    _wait_pages(cidx_hbm, ibuf, slot, sem)
    cw = chunk_pages * page
    for c in range(npages // chunk_pages):
        kt = jnp.concatenate([ibuf[slot, c * chunk_pages + j] for j in range(chunk_pages)], axis=1)
        lg = jnp.dot(qi, kt.astype(BF16), preferred_element_type=F32)
        key_ref[:, c * cw:(c + 1) * cw] = _sortable(head_sum(lg))

    lg = lax.dot_general(qi, _pad_rows(kin_ref[...], LANES), _NT, preferred_element_type=F32)
    sc = head_sum(lg)
    t_io = lax.broadcasted_iota(I32, sc.shape, 0)
    j_io = lax.broadcasted_iota(I32, sc.shape, 1)
    key_ref[:, npages * page:] = jnp.where(j_io <= t_io, _sortable(sc), INT_MIN)


def _sample_scores(page_table, qi, wi, kib, cache_idx_t, td, chunk_pages=4):
    bd, npages = page_table.shape
    page = cache_idx_t.shape[2]
    nk = npages * page + LANES
    grid_spec = pltpu.PrefetchScalarGridSpec(
        num_scalar_prefetch=1, grid=(bd,),
        in_specs=[
            pl.BlockSpec((td, IDX_HEADS * IDX_DIM), lambda b, pt: (b, 0)),
            pl.BlockSpec((td, IDX_HEADS), lambda b, pt: (b, 0)),
            pl.BlockSpec((td, IDX_DIM), lambda b, pt: (b, 0)),
            pl.BlockSpec(memory_space=pl.ANY),
        ],
        out_specs=pl.BlockSpec((td, nk), lambda b, pt: (b, 0)),
        scratch_shapes=[pltpu.VMEM((2, npages, IDX_DIM, page), F32), pltpu.SemaphoreType.DMA((2,))],
    )
    return pl.pallas_call(
        functools.partial(_sample_score_kernel, chunk_pages=chunk_pages),
        grid_spec=grid_spec, out_shape=jax.ShapeDtypeStruct((bd * td, nk), I32),
        compiler_params=_cparams(("arbitrary",)), name="sample_scores",
    )(page_table, qi, wi, kib, cache_idx_t)


def _select_kernel(key_ref, bias_ref, *, topk):
    rows, nk = key_ref.shape
    ntile = nk // LANES

    def count(pred):
        acc = jnp.zeros((rows, LANES), I32)
        for c in range(ntile):
            acc = acc + pred(key_ref[:, c * LANES:(c + 1) * LANES], c).astype(I32)
        return jnp.sum(acc, axis=-1, keepdims=True)

    kmin = jnp.full((rows, LANES), 2 ** 31 - 1, I32)
    kmax = jnp.full((rows, LANES), INT_MIN, I32)
    nvalid = jnp.zeros((rows, LANES), I32)
    for c in range(ntile):
        k = key_ref[:, c * LANES:(c + 1) * LANES]
        valid = k != INT_MIN
        kmax = jnp.maximum(kmax, k)
        kmin = jnp.minimum(kmin, jnp.where(valid, k, jnp.int32(2 ** 31 - 1)))
        nvalid = nvalid + valid.astype(I32)
    thr = _find_separator(lambda mid: count(lambda k, c: k >= mid),
                          jnp.min(kmin, axis=-1, keepdims=True), jnp.max(kmax, axis=-1, keepdims=True) + 1,
                          jnp.sum(nvalid, axis=-1, keepdims=True), topk)

    n_gt = count(lambda k, c: k > thr)
    n_eq = count(lambda k, c: k == thr)
    need = topk - n_gt
    l_io = lax.broadcasted_iota(I32, (rows, LANES), 1)
    n_idx_bits = max(1, nk.bit_length())

    def tie_search(_):
        def body(p, m):
            cand = m | jnp.left_shift(jnp.int32(1), n_idx_bits - 1 - p)
            below = count(lambda k, c: (k == thr) & (l_io + c * LANES < cand))
            return jnp.where(below < need, cand, m)
        return lax.fori_loop(0, n_idx_bits, body, jnp.zeros((rows, 1), I32))

    excess = jnp.max(jnp.where((need > 0) & (thr > INT_MIN), n_eq - need, 0)) > 0
    idx_max = lax.cond(excess, tie_search, lambda _: jnp.full((rows, 1), nk, I32), 0)

    for c in range(ntile):
        k = key_ref[:, c * LANES:(c + 1) * LANES]
        sel = ((k > thr) | ((k == thr) & (l_io + c * LANES <= idx_max))) & (k != INT_MIN)
        bias_ref[:, c * LANES:(c + 1) * LANES] = jnp.where(sel, 0.0, NEG).astype(F32)


def _select(keys, topk, rows):
    n, nk = keys.shape
    return pl.pallas_call(
        functools.partial(_select_kernel, topk=topk),
        grid=(n // rows,),
        in_specs=[pl.BlockSpec((rows, nk), lambda i: (i, 0))],
        out_specs=pl.BlockSpec((rows, nk), lambda i: (i, 0)),
        out_shape=jax.ShapeDtypeStruct((n, nk), F32),
        compiler_params=_cparams(("parallel",)), name="sample_select",
    )(keys)


def _sample_attn_kernel(pt_ref, q_ref, bias_ref, kn_ref, vn_ref, ck_hbm, cv_hbm, a_ref,
                        kbuf, vbuf, s_ref, ksem, vsem, *, chunk_pages):
    b = pl.program_id(0)
    nb = pl.num_programs(0)
    slot = b % 2
    npages, page = kbuf.shape[1], kbuf.shape[4]
    td = q_ref.shape[0]
    half = GROUP * td

    @pl.when(b == 0)
    def _():
        _start_pages(pt_ref, b, ck_hbm, kbuf, 0, ksem)
        _start_pages(pt_ref, b, cv_hbm, vbuf, 0, vsem)

    @pl.when(b + 1 < nb)
    def _():
        _start_pages(pt_ref, b + 1, ck_hbm, kbuf, 1 - slot, ksem)
        _start_pages(pt_ref, b + 1, cv_hbm, vbuf, 1 - slot, vsem)

    q = _rows_by_head(q_ref[...], HEAD_DIM)
    qg = [q[g * half:(g + 1) * half] for g in range(N_KV_HEADS)]
    cw = chunk_pages * page
    nchunk = npages // chunk_pages
    new_lo = npages * page
    kn = _pad_rows(kn_ref[...], LANES).astype(BF16)
    vn = _pad_rows(vn_ref[...], LANES).astype(BF16)

    def chunk_t(buf, c, g):
        return jnp.concatenate([buf[slot, c * chunk_pages + j, g] for j in range(chunk_pages)],
                               axis=1).astype(BF16)

    def tile_bias(lo, n):
        return jnp.concatenate([bias_ref[:, lo:lo + n]] * GROUP, axis=0)

    _wait_pages(ck_hbm, kbuf, slot, ksem)
    ms = []
    for g in range(N_KV_HEADS):
        rows = slice(g * half, (g + 1) * half)
        mrun = jnp.full((half, LANES), NEG, F32)
        for c in range(nchunk):
            s = jnp.dot(qg[g], chunk_t(kbuf, c, g), preferred_element_type=F32) + tile_bias(c * cw, cw)
            s_ref[rows, c * cw:(c + 1) * cw] = s
            for j in range(cw // LANES):
                mrun = jnp.maximum(mrun, s[:, j * LANES:(j + 1) * LANES])
        s_new = lax.dot_general(qg[g], kn[:, g * HEAD_DIM:(g + 1) * HEAD_DIM], _NT,
                                preferred_element_type=F32) + tile_bias(new_lo, LANES)
        s_ref[rows, new_lo:] = s_new
        ms.append(jnp.max(jnp.maximum(mrun, s_new), axis=-1, keepdims=True))

    _wait_pages(cv_hbm, vbuf, slot, vsem)
    for g in range(N_KV_HEADS):
        rows = slice(g * half, (g + 1) * half)
        lrun = jnp.zeros((half, LANES), F32)
        acc = jnp.zeros((half, HEAD_DIM), F32)
        for c in range(nchunk):
            p = jnp.exp2(s_ref[rows, c * cw:(c + 1) * cw] - ms[g])
            for j in range(cw // LANES):
                lrun = lrun + p[:, j * LANES:(j + 1) * LANES]
            acc = acc + lax.dot_general(p.astype(BF16), chunk_t(vbuf, c, g), _NT,
                                        preferred_element_type=F32)
        p = jnp.exp2(s_ref[rows, new_lo:] - ms[g])
        lrun = lrun + p
        acc = acc + jnp.dot(p.astype(BF16), vn[:, g * HEAD_DIM:(g + 1) * HEAD_DIM],
                            preferred_element_type=F32)
        o = acc / jnp.sum(lrun, axis=-1, keepdims=True)
        for hh in range(GROUP):
            h = GROUP * g + hh
            a_ref[:, h * HEAD_DIM:(h + 1) * HEAD_DIM] = o[hh * td:(hh + 1) * td, :].astype(BF16)


def _sample_attention(page_table, q, bias, k_new, v_new, cache_kt, cache_vt, td, chunk_pages=4):
    bd, npages = page_table.shape
    page = cache_kt.shape[3]
    nk = bias.shape[1]
    rowb = lambda b, pt: (b, 0)
    grid_spec = pltpu.PrefetchScalarGridSpec(
        num_scalar_prefetch=1, grid=(bd,),
        in_specs=[
            pl.BlockSpec((td, ATTN_WIDTH), rowb),
            pl.BlockSpec((td, nk), rowb),
            pl.BlockSpec((td, KV_WIDTH), rowb),
            pl.BlockSpec((td, KV_WIDTH), rowb),
            pl.BlockSpec(memory_space=pl.ANY),
            pl.BlockSpec(memory_space=pl.ANY),
        ],
        out_specs=pl.BlockSpec((td, ATTN_WIDTH), rowb),
        scratch_shapes=[
            pltpu.VMEM((2, npages, N_KV_HEADS, HEAD_DIM, page), F32),
            pltpu.VMEM((2, npages, N_KV_HEADS, HEAD_DIM, page), F32),
            pltpu.VMEM((N_HEADS * td, nk), F32),
            pltpu.SemaphoreType.DMA((2,)),
            pltpu.SemaphoreType.DMA((2,)),
        ],
    )
    return pl.pallas_call(
        functools.partial(_sample_attn_kernel, chunk_pages=chunk_pages),
        grid_spec=grid_spec, out_shape=jax.ShapeDtypeStruct((bd * td, ATTN_WIDTH), BF16),
        compiler_params=_cparams(("arbitrary",)), name="sample_dsa",
    )(page_table, q, bias, k_new, v_new, cache_kt, cache_vt)


def _layer_weights(i, g_mix, w_in, g_idx_k, b_idx_k, w_dw, b_dw, g_conv_ln, b_conv_ln, w_out, g_ffn, w_up,
                   w_ffn_conv, b_ffn_conv, w_down, g_ple, w_ple_gate, w_ple, g_final):
    d = w_in.shape[1]
    w = w_in[i]
    n_qkvi = C_KI + IDX_DIM + IDX_HEADS
    pad = jnp.zeros((d, LANES - IDX_DIM - IDX_HEADS), w.dtype)
    w_comb = jnp.concatenate([w[:, :n_qkvi], pad, w[:, n_qkvi:]], axis=1).astype(BF16)
    zpad = jnp.zeros((LANES - IDX_DIM,), F32)
    gik = jnp.concatenate([g_idx_k[i], zpad])[None]
    bik = jnp.concatenate([b_idx_k[i], zpad])[None]
    proj = (g_mix[i][None], w_comb, gik, bik)

    wo = w_out[i].astype(BF16)
    conv = (w_dw[i], b_dw[i][None], g_conv_ln[i][None], b_conv_ln[i][None], wo[:ATTN_WIDTH], wo[ATTN_WIDTH:])

    d_ff = w_down.shape[1]
    nch = d_ff // FFN_CW
    ffn = (g_ffn[i][None], _ffn_chunks(w_up[i].astype(BF16), nch), _ffn_chunks(w_ffn_conv[i], nch),
           _ffn_chunks(b_ffn_conv[i][None], nch), w_down[i].astype(BF16).reshape(nch, FFN_CW, d),
           g_ple[i][None], w_ple_gate[i].astype(BF16), w_ple[i].astype(BF16), g_final[None])
    return proj, conv, ffn


def _ffn_chunks(m, nch):
    rows = m.shape[0]
    return m.reshape(rows, 2, nch, FFN_CW).transpose(2, 0, 1, 3).reshape(nch, rows, 2 * FFN_CW)


def _ffn_unchunk(st):
    nch, rows, cw2 = st.shape
    return st.reshape(nch, rows, 2, cw2 // 2).transpose(1, 2, 0, 3).reshape(rows, nch * cw2)


def kernel(x_prompt, x_sample, p_prompt, p_sample, cache_k, cache_v, cache_idx_k, state_conv, state_ffn_conv,
           page_table, g_mix, w_in, g_idx_k, b_idx_k, w_dw, b_dw, g_conv_ln, b_conv_ln, w_out, g_ffn, w_up,
           w_ffn_conv, b_ffn_conv, w_down, g_ple, w_ple_gate, w_ple, g_final):
    B, S, D = x_prompt.shape
    Bd, Td, _ = x_sample.shape
    depth = w_in.shape[0]
    n_pages = page_table.shape[1]
    page = cache_k.shape[2]
    past_len = n_pages * page
    d_ff = w_down.shape[1]
    nch = d_ff // FFN_CW
    topk_prompt = min(TOPK_MAX, S // 4)
    topk_sample = min(TOPK_MAX, (past_len + Td) // 4)
    assert S % KC == 0 and Td == SUBLANES and d_ff % FFN_CW == 0 and page == LANES

    tm_p = KC
    BL = min(32, Bd)
    G = Bd // BL
    tm_s = Td * BL

    tabs_p = _rope_tables(jnp.arange(S, dtype=I32))
    pos_s = past_len + jnp.arange(Td, dtype=I32)
    tabs_s = tuple(jnp.tile(t, (Bd, 1)) for t in _rope_tables(pos_s))

    def to_tm(a):
        return a.reshape(G, BL, Td, a.shape[-1]).transpose(0, 2, 1, 3).reshape(G * tm_s, a.shape[-1])

    def from_tm(a):
        return a.reshape(G, Td, BL, a.shape[-1]).transpose(0, 2, 1, 3).reshape(Bd, Td, a.shape[-1])

    hp = x_prompt.reshape(B * S, D)
    hs = x_sample.reshape(Bd * Td, D)
    outs = {k: [] for k in ("kp", "vp", "ip", "cp", "fp", "ks", "vs", "is", "cs", "fs")}
    for i in range(depth):
        last = i == depth - 1
        proj_w, conv_w, ffn_w = _layer_weights(
            i, g_mix, w_in, g_idx_k, b_idx_k, w_dw, b_dw, g_conv_ln, b_conv_ln, w_out, g_ffn, w_up,
            w_ffn_conv, b_ffn_conv, w_down, g_ple, w_ple_gate, w_ple, g_final)

        q, qit, kt, vt, kit, ktb, vb, kib, wit, glu = _project(
            hp, tabs_p, S // tm_p, *proj_w, tm=tm_p, prompt=True, batch=B)
        a = _prompt_attention(q, qit, wit, kib, ktb, vb, B, S, topk_prompt)
        nt = S // tm_p
        halo_rows = 32
        halo_spec = pl.BlockSpec(
            (halo_rows, CONV_CH), lambda b, t: (jnp.maximum((b * nt + t) * (tm_p // halo_rows) - 1, 0), 0))
        h1 = _conv_out(glu, glu, halo_spec, a, hp, *conv_w, groups=B, nt=nt, tm=tm_p, shift=1, zero_first=True)
        st0 = jnp.zeros((B, nch, SUBLANES, 2 * FFN_CW), F32)
        hp, st = _ffn(h1, p_prompt[i].reshape(B * S, -1), st0, ffn_w, groups=B, nt=nt, tm=tm_p,
                      shift=1, halo=SUBLANES, final=last)
        outs["kp"].append(kt.transpose(0, 3, 1, 2))
        outs["vp"].append(vt.transpose(0, 3, 1, 2))
        outs["ip"].append(kit.transpose(0, 2, 1))
        outs["cp"].append(glu.reshape(B, S, CONV_CH)[:, S - (CONV_W - 1):])
        outs["fp"].append(jax.vmap(_ffn_unchunk)(st)[:, SUBLANES - (FFN_CONV_W - 1):])

        tm_sp = min(512, Bd * Td)
        q, qi, k, v, ki, kib, wi, glu = _project(hs, tabs_s, Bd * Td // tm_sp, *proj_w, tm=tm_sp, prompt=False)
        keys = _sample_scores(page_table, qi, wi, kib, cache_idx_k[i].transpose(0, 2, 1), Td)
        bias = _select(keys, topk_sample, rows=min(64, Bd * Td))
        a = _sample_attention(page_table, q, bias, k, v,
                              cache_k[i].transpose(0, 2, 3, 1), cache_v[i].transpose(0, 2, 3, 1), Td)
        sc = state_conv[i]
        halo_s = sc.reshape(G, BL, CONV_W - 1, CONV_CH).transpose(0, 2, 1, 3).reshape(G * (CONV_W - 1) * BL, CONV_CH)
        halo_spec = pl.BlockSpec(((CONV_W - 1) * BL, CONV_CH), lambda g, t: (g, 0))
        h1 = _conv_out(to_tm(glu), halo_s, halo_spec, to_tm(a), to_tm(hs), *conv_w, groups=G, nt=1, tm=tm_s,
                       shift=BL, zero_first=False)
        sf = state_ffn_conv[i]
        sf = sf.reshape(G, BL, FFN_CONV_W - 1, 2 * d_ff).transpose(0, 2, 1, 3).reshape(G, (FFN_CONV_W - 1) * BL, 2 * d_ff)
        st0 = jax.vmap(lambda s: _ffn_chunks(s, nch))(sf)
        hs_tm, st = _ffn(h1, to_tm(p_sample[i].reshape(Bd * Td, -1)), st0, ffn_w, groups=G, nt=1, tm=tm_s,
                         shift=BL, halo=(FFN_CONV_W - 1) * BL, final=last)
        hs = from_tm(hs_tm).reshape(Bd * Td, D)
        fs = jax.vmap(_ffn_unchunk)(st)
        fs = fs.reshape(G, FFN_CONV_W - 1, BL, 2 * d_ff).transpose(0, 2, 1, 3).reshape(Bd, FFN_CONV_W - 1, 2 * d_ff)
        outs["ks"].append(k.reshape(Bd, Td, N_KV_HEADS, HEAD_DIM))
        outs["vs"].append(v.reshape(Bd, Td, N_KV_HEADS, HEAD_DIM))
        outs["is"].append(ki.reshape(Bd, Td, IDX_DIM))
        outs["cs"].append(jnp.concatenate([sc, glu.reshape(Bd, Td, CONV_CH)], axis=1)[:, Td:])
        outs["fs"].append(fs)

    y_prompt = hp.reshape(B, S, D)
    y_sample = hs.reshape(Bd, Td, D)
    st = lambda name: jnp.stack(outs[name])
    return (y_prompt, y_sample, st("kp"), st("vp"), st("ip"), st("cp"), st("fp"),
            st("ks"), st("vs"), st("is"), st("cs"), st("fs"))
```
